```python
import math
import jax, jax.numpy as jnp
from jax import lax
import numpy as np

D_MODEL = 1024
BATCH = 8
SEQ = 4096
DEPTH = 1

HEAD_DIM = 64
N_ATTN_HEADS = 8
ATTN_WIDTH = N_ATTN_HEADS * HEAD_DIM
N_RWKV_HEADS = 8
RWKV_WIDTH = N_RWKV_HEADS * HEAD_DIM
MIX_WIDTH = ATTN_WIDTH + RWKV_WIDTH
RWKV_DECAY_LORA = 32
RWKV_AAA_LORA = 32
RWKV_GATE_LORA = 96
RWKV_SHIFT_WIDTH = 3 * RWKV_WIDTH + RWKV_DECAY_LORA + RWKV_AAA_LORA + RWKV_GATE_LORA
IN_PROJ_WIDTH = 3 * ATTN_WIDTH + RWKV_SHIFT_WIDTH
RWKV_DECAY_SCALE = math.exp(-0.5)
RWKV_GN_EPS = HEAD_DIM * 1e-5
MOBA_BLOCK = 256
MOBA_TOPK = 3
MOBA_QCHUNK = 32
N_EXPERTS = 32
TOP_K = 4
D_EXPERT = D_MODEL
SWIGLU_LIMIT = 7.0
SWIGLU_ALPHA = 1.702
MOE_BLOCK = 256
NORM_EPS = 1e-6
NEG_INF = -1e30

kernel_name = "hymba_moba_rwkv7_gptoss_moe_block"


def rms_norm(x, g):
    xf = x.astype(jnp.float32)
    return xf * lax.rsqrt(jnp.mean(xf * xf, axis=-1, keepdims=True) + NORM_EPS) * g.astype(jnp.float32)


def alibi_slopes(n_heads):
    return 2.0 ** (-8.0 * jnp.arange(1, n_heads + 1, dtype=jnp.float32) / n_heads)


def moba_attention(q, k, v, slopes):
    B, S, H, Dh = q.shape
    Sp = -(-S // MOBA_BLOCK) * MOBA_BLOCK
    pad = ((0, 0), (0, Sp - S), (0, 0), (0, 0))
    q, k, v = [jnp.pad(t, pad).transpose(0, 2, 1, 3) for t in (q, k, v)]
    nb = Sp // MOBA_BLOCK
    n_sel = min(MOBA_TOPK, nb)
    n_chunks = Sp // MOBA_QCHUNK
    kb = k.reshape(B, H, nb, MOBA_BLOCK, Dh)
    vb = v.reshape(B, H, nb, MOBA_BLOCK, Dh)
    k_mean = kb.mean(axis=3)
    q_chunks = q.reshape(B, H, n_chunks, MOBA_QCHUNK, Dh).transpose(2, 0, 1, 3, 4)
    scale = Dh ** -0.5
    b_idx = jnp.arange(B)[:, None, None, None]
    h_idx = jnp.arange(H)[None, :, None, None]
    key_off = jnp.arange(MOBA_BLOCK)

    def attend_chunk(args):
        qq, ci = args
        t = ci * MOBA_QCHUNK + jnp.arange(MOBA_QCHUNK)
        own = (ci * MOBA_QCHUNK) // MOBA_BLOCK
        gate = jnp.einsum('bhqd,bhnd->bhqn', qq, k_mean)
        gate = jnp.where(jnp.arange(nb) < own, gate, NEG_INF)
        _, sel = lax.top_k(gate, n_sel)
        sel_valid = jnp.arange(n_sel) < own
        k_sel = kb[b_idx, h_idx, sel]
        v_sel = vb[b_idx, h_idx, sel]
        pos_sel = sel[..., None] * MOBA_BLOCK + key_off
        dist_sel = (t[None, None, :, None, None] - pos_sel).astype(jnp.float32)
        s_sel = (jnp.einsum('bhqd,bhqnkd->bhqnk', qq, k_sel) * scale
                 - slopes[None, :, None, None, None] * dist_sel)
        s_sel = jnp.where(sel_valid[:, None], s_sel, NEG_INF)
        k_own = lax.dynamic_index_in_dim(kb, own, axis=2, keepdims=False)
        v_own = lax.dynamic_index_in_dim(vb, own, axis=2, keepdims=False)
        pos_own = own * MOBA_BLOCK + key_off
        dist_own = (t[:, None] - pos_own[None, :]).astype(jnp.float32)
        s_own = (jnp.einsum('bhqd,bhkd->bhqk', qq, k_own) * scale
                 - slopes[None, :, None, None] * dist_own[None, None])
        s_own = jnp.where(dist_own >= 0, s_own, NEG_INF)
        scores = jnp.concatenate([s_sel.reshape(B, H, MOBA_QCHUNK, n_sel * MOBA_BLOCK), s_own], axis=-1)
        p = jax.nn.softmax(scores.astype(jnp.float32), axis=-1)
        p_sel = p[..., :n_sel * MOBA_BLOCK].reshape(B, H, MOBA_QCHUNK, n_sel, MOBA_BLOCK)
        p_own = p[..., n_sel * MOBA_BLOCK:]
        return (jnp.einsum('bhqnk,bhqnkd->bhqd', p_sel, v_sel)
                + jnp.einsum('bhqk,bhkd->bhqd', p_own, v_own))

    out = lax.map(attend_chunk, (q_chunks, jnp.arange(n_chunks)))
    out = out.transpose(1, 0, 3, 2, 4).reshape(B, Sp, H, Dh)
    return out[:, :S]


def rwkv7_time_mix(xr, mu, w0, w2, a0, a2, g2, k_k, k_a, r_k, ln_g, ln_b):
    B, S, _ = xr.shape
    H, N = N_RWKV_HEADS, HEAD_DIM
    x_prev = jnp.pad(xr, ((0, 0), (1, 0), (0, 0)))[:, :S]
    xs = xr + (x_prev - xr) * mu
    o1 = RWKV_WIDTH
    o2 = 2 * RWKV_WIDTH
    o3 = 3 * RWKV_WIDTH
    o4 = o3 + RWKV_DECAY_LORA
    o5 = o4 + RWKV_AAA_LORA
    r, k, v, dw, da, dg = jnp.split(xs, [o1, o2, o3, o4, o5], axis=-1)
    w = jnp.exp(-RWKV_DECAY_SCALE * jax.nn.sigmoid(w0 + jnp.tanh(dw) @ w2))
    a = jax.nn.sigmoid(a0 + da @ a2)
    g = jax.nn.sigmoid(dg) @ g2
    heads = lambda t: t.reshape(B, S, H, N)
    kk = heads(k * k_k)
    kk = kk * lax.rsqrt(jnp.maximum(jnp.sum(kk * kk, axis=-1, keepdims=True), 1e-24))
    k = k * (1.0 + (a - 1.0) * k_a)
    r, w, k, v, a = [heads(t) for t in (r, w, k, v, a)]
    b_vec = kk * a
    seq_first = lambda t: t.transpose(1, 0, 2, 3)

    def step(state, inp):
        r_t, w_t, k_t, v_t, kk_t, b_t = inp
        sa = jnp.einsum('bhvk,bhk->bhv', state, -kk_t)
        state = (state * w_t[:, :, None, :] + sa[..., None] * b_t[:, :, None, :]
                 + v_t[..., None] * k_t[:, :, None, :])
        return state, jnp.einsum('bhvk,bhk->bhv', state, r_t)

    state0 = jnp.zeros((B, H, N, N), jnp.float32)
    xs_scan = tuple(seq_first(t.astype(jnp.float32)) for t in (r, w, k, v, kk, b_vec))
    _, o = lax.scan(step, state0, xs_scan)
    o = seq_first(o)
    mean = jnp.mean(o, axis=-1, keepdims=True)
    var = jnp.mean(jnp.square(o - mean), axis=-1, keepdims=True)
    o = ((o - mean) * lax.rsqrt(var + RWKV_GN_EPS)).reshape(B, S, RWKV_WIDTH) * ln_g + ln_b
    bonus = jnp.sum(r * k * r_k, axis=-1, keepdims=True) * v
    return (o + bonus.reshape(B, S, RWKV_WIDTH)) * g


def moe_ffn(xn, w_router, b_router, w_gate, b_gate, w_up, b_up, w_down, b_down):
    T, D = xn.shape
    logits = (xn @ w_router + b_router).astype(jnp.float32)
    top_val, top_idx = lax.top_k(logits, TOP_K)
    gates = jax.nn.softmax(top_val, axis=-1)
    A = T * TOP_K
    e_flat = top_idx.reshape(A)
    tok_flat = jnp.arange(A) // TOP_K
    g_flat = gates.reshape(A)
    order = jnp.argsort(e_flat)
    e_sorted = e_flat[order]
    tok_sorted = tok_flat[order]
    g_sorted = g_flat[order]
    counts = jnp.bincount(e_flat, length=N_EXPERTS)
    padded = ((counts + MOE_BLOCK - 1) // MOE_BLOCK) * MOE_BLOCK
    pad_end = jnp.cumsum(padded)
    pad_start = pad_end - padded
    grp_start = jnp.cumsum(counts) - counts
    dest = pad_start[e_sorted] + (jnp.arange(A) - grp_start[e_sorted])
    n_blocks = -(-A // MOE_BLOCK) + N_EXPERTS
    P = n_blocks * MOE_BLOCK
    slot_tok = jnp.full((P,), T, dtype=jnp.int32).at[dest].set(tok_sorted.astype(jnp.int32))
    block_expert = jnp.minimum(
        jnp.searchsorted(pad_end, jnp.arange(n_blocks) * MOE_BLOCK, side='right'), N_EXPERTS - 1)
    x_pad = jnp.concatenate([xn, jnp.zeros((1, D), xn.dtype)], axis=0)

    def expert_block(args):
        toks, e = args
        h = x_pad[toks]
        gate = jnp.minimum(h @ w_gate[e] + b_gate[e], SWIGLU_LIMIT)
        up = jnp.clip(h @ w_up[e] + b_up[e], -SWIGLU_LIMIT, SWIGLU_LIMIT)
        glu = gate * jax.nn.sigmoid(SWIGLU_ALPHA * gate)
        return ((up + 1.0) * glu) @ w_down[e] + b_down[e]

    y_slots = lax.map(expert_block, (slot_tok.reshape(n_blocks, MOE_BLOCK), block_expert)).reshape(P, D)
    return jax.ops.segment_sum(y_slots[dest] * g_sorted[:, None], tok_sorted, num_segments=T)


def setup_inputs(seed: int = 0) -> dict:
    key = jax.random.key(seed)
    ks = jax.random.split(key, 32)
    L, D, E, F = DEPTH, D_MODEL, N_EXPERTS, D_EXPERT
    nrm = lambda k, shape, s: jax.random.normal(k, shape, jnp.float32) * s
    return {
        "x": nrm(ks[0], (BATCH, SEQ, D), 1.0),
        "norm_mix_g": 1.0 + nrm(ks[1], (L, D), 0.05),
        "w_in": nrm(ks[2], (L, D, IN_PROJ_WIDTH), D ** -0.5),
        "attn_norm_g": 1.0 + nrm(ks[3], (L, ATTN_WIDTH), 0.05),
        "rwkv_mu": jax.random.uniform(ks[4], (L, RWKV_SHIFT_WIDTH), jnp.float32),
        "rwkv_w0": nrm(ks[5], (L, RWKV_WIDTH), 0.5),
        "rwkv_w2": nrm(ks[6], (L, RWKV_DECAY_LORA, RWKV_WIDTH), 0.1),
        "rwkv_a0": nrm(ks[7], (L, RWKV_WIDTH), 0.1),
        "rwkv_a2": nrm(ks[8], (L, RWKV_AAA_LORA, RWKV_WIDTH), 0.1),
        "rwkv_g2": nrm(ks[9], (L, RWKV_GATE_LORA, RWKV_WIDTH), RWKV_GATE_LORA ** -0.5),
        "rwkv_k_k": 0.85 + nrm(ks[10], (L, RWKV_WIDTH), 0.05),
        "rwkv_k_a": 1.0 + nrm(ks[11], (L, RWKV_WIDTH), 0.05),
        "rwkv_r_k": nrm(ks[12], (L, N_RWKV_HEADS, HEAD_DIM), 0.1),
        "rwkv_ln_g": 1.0 + nrm(ks[13], (L, RWKV_WIDTH), 0.05),
        "rwkv_ln_b": nrm(ks[14], (L, RWKV_WIDTH), 0.01),
        "w_out": nrm(ks[15], (L, MIX_WIDTH, D), MIX_WIDTH ** -0.5),
        "norm_ffn_g": 1.0 + nrm(ks[16], (L, D), 0.05),
        "w_router": nrm(ks[17], (L, D, E), D ** -0.5),
        "b_router": nrm(ks[18], (L, E), 0.01),
        "moe_w_gate": nrm(ks[19], (L, E, D, F), D ** -0.5),
        "moe_b_gate": nrm(ks[20], (L, E, F), 0.01),
        "moe_w_up": nrm(ks[21], (L, E, D, F), D ** -0.5),
        "moe_b_up": nrm(ks[22], (L, E, F), 0.01),
        "moe_w_down": nrm(ks[23], (L, E, F, D), F ** -0.5),
        "moe_b_down": nrm(ks[24], (L, E, D), 0.01),
        "norm_final_g": 1.0 + nrm(ks[25], (D,), 0.05),
    }


def reference(x, norm_mix_g, w_in, attn_norm_g, rwkv_mu, rwkv_w0, rwkv_w2, rwkv_a0, rwkv_a2,
              rwkv_g2, rwkv_k_k, rwkv_k_a, rwkv_r_k, rwkv_ln_g, rwkv_ln_b, w_out, norm_ffn_g,
              w_router, b_router, moe_w_gate, moe_b_gate, moe_w_up, moe_b_up, moe_w_down,
              moe_b_down, norm_final_g):
    B, S, D = x.shape
    slopes = alibi_slopes(N_ATTN_HEADS)
    h = x.astype(jnp.float32)
    for l in range(DEPTH):
        xn = rms_norm(h, norm_mix_g[l])
        proj = xn @ w_in[l]
        q = proj[..., :ATTN_WIDTH].reshape(B, S, N_ATTN_HEADS, HEAD_DIM)
        k = proj[..., ATTN_WIDTH:2 * ATTN_WIDTH].reshape(B, S, N_ATTN_HEADS, HEAD_DIM)
        v = proj[..., 2 * ATTN_WIDTH:3 * ATTN_WIDTH].reshape(B, S, N_ATTN_HEADS, HEAD_DIM)
        o_att = moba_attention(q, k, v, slopes)
        o_att = o_att * lax.rsqrt(jnp.mean(o_att * o_att, axis=-1, keepdims=True) + NORM_EPS)
        o_att = o_att.reshape(B, S, ATTN_WIDTH) * attn_norm_g[l]
        o_rwkv = rwkv7_time_mix(proj[..., 3 * ATTN_WIDTH:], rwkv_mu[l], rwkv_w0[l], rwkv_w2[l],
                                rwkv_a0[l], rwkv_a2[l], rwkv_g2[l], rwkv_k_k[l], rwkv_k_a[l],
                                rwkv_r_k[l], rwkv_ln_g[l], rwkv_ln_b[l])
        h = h + jnp.concatenate([o_att, o_rwkv], axis=-1) @ w_out[l]
        xn2 = rms_norm(h, norm_ffn_g[l]).reshape(B * S, D)
        y = moe_ffn(xn2, w_router[l], b_router[l], moe_w_gate[l], moe_b_gate[l],
                    moe_w_up[l], moe_b_up[l], moe_w_down[l], moe_b_down[l])
        h = h + y.reshape(B, S, D)
    return rms_norm(h, norm_final_g).astype(x.dtype)
```

```python
import functools
import math

import jax
import jax.numpy as jnp
from jax import lax
from jax.experimental import pallas as pl
from jax.experimental.pallas import tpu as pltpu

F32 = jnp.float32
BF16 = jnp.bfloat16

HEAD_DIM = 64
N_HEADS = 8
GROUP_WIDTH = N_HEADS * HEAD_DIM
MOBA_BLOCK = 256
MOBA_TOPK = 3
N_EXPERTS = 32
TOP_K = 4
MOE_BLOCK = 256
SWIGLU_LIMIT = 7.0
SWIGLU_ALPHA = 1.702
NORM_EPS = 1e-6
NEG_INF = -1e30
RWKV_DECAY_SCALE = math.exp(-0.5)
RWKV_GN_EPS = HEAD_DIM * 1e-5
RWKV_LORA = (32, 32, 96)
RWKV_LORA_PAD = 256
RWKV_CHUNK = 64

VMEM_LIMIT = 48 * 1024 * 1024


def _cparams(*sem):
    return pltpu.CompilerParams(dimension_semantics=sem, vmem_limit_bytes=VMEM_LIMIT)


def _dot(a, b):
    return jnp.dot(a, b, preferred_element_type=F32)


def _dot_nt(a, b):
    return lax.dot_general(a, b, (((1,), (1,)), ((), ())), preferred_element_type=F32)


def _dot_tn(a, b):
    return lax.dot_general(a, b, (((0,), (0,)), ((), ())), preferred_element_type=F32)


def _split2(x):
    hi = x.astype(BF16)
    lo = (x - hi.astype(F32)).astype(BF16)
    return hi, lo


def _inproj_kernel(x_ref, g_ref, wq_ref, wk_ref, wvt_ref, wr_ref, mu_ref,
                   q_ref, k_ref, vt_ref, kmean_ref, xs_ref, carry_ref, *, tiles_per_seq):
    i = pl.program_id(0)
    x = x_ref[...]
    xn = x * lax.rsqrt(jnp.mean(x * x, axis=-1, keepdims=True) + NORM_EPS) * g_ref[...]
    xb = xn.astype(BF16)
    q_ref[...] = (_dot(xb, wq_ref[...]) * (HEAD_DIM ** -0.5)).astype(BF16)
    k = _dot(xb, wk_ref[...])
    k_ref[...] = k.astype(BF16)
    kmean_ref[0] = jnp.mean(k, axis=0, keepdims=True)
    vt_ref[0] = _dot_nt(wvt_ref[...], xb).astype(BF16)
    pr = _dot(xb, wr_ref[...])

    @pl.when(i % tiles_per_seq == 0)
    def _():
        carry_ref[...] = jnp.zeros_like(carry_ref)

    tm = pr.shape[0]
    rolled = pltpu.roll(pr, 1, 0)
    row = lax.broadcasted_iota(jnp.int32, pr.shape, 0)
    prev = jnp.where(row == 0, carry_ref[...], rolled)
    carry_ref[...] = pr[tm - 1:tm, :]
    xs_ref[...] = pr + (prev - pr) * mu_ref[...]


def _in_projection(x2, norm_g, w_in, mu, batch, seq):
    T, D = x2.shape
    tm = MOBA_BLOCK
    nb = seq // tm
    GW = GROUP_WIDTH
    wq = w_in[:, :GW].astype(BF16)
    wk = w_in[:, GW:2 * GW].astype(BF16)
    wvt = w_in[:, 2 * GW:3 * GW].T.astype(BF16)
    rw = 3 * GW + RWKV_LORA_PAD
    n_r = w_in.shape[1] - 3 * GW
    wr = jnp.pad(w_in[:, 3 * GW:], ((0, 0), (0, rw - n_r))).astype(BF16)
    mu_p = jnp.pad(mu, (0, rw - n_r)).reshape(1, rw)
    const = lambda i: (0, 0)
    return pl.pallas_call(
        functools.partial(_inproj_kernel, tiles_per_seq=nb),
        grid=(T // tm,),
        in_specs=[
            pl.BlockSpec((tm, D), lambda i: (i, 0)),
            pl.BlockSpec((1, D), const),
            pl.BlockSpec((D, GW), const),
            pl.BlockSpec((D, GW), const),
            pl.BlockSpec((GW, D), const),
            pl.BlockSpec((D, rw), const),
            pl.BlockSpec((1, rw), const),
        ],
        out_specs=[
            pl.BlockSpec((tm, GW), lambda i: (i, 0)),
            pl.BlockSpec((tm, GW), lambda i: (i, 0)),
            pl.BlockSpec((1, GW, tm), lambda i: (i, 0, 0)),
            pl.BlockSpec((1, 1, GW), lambda i: (i, 0, 0)),
            pl.BlockSpec((tm, rw), lambda i: (i, 0)),
        ],
        out_shape=[
            jax.ShapeDtypeStruct((T, GW), BF16),
            jax.ShapeDtypeStruct((T, GW), BF16),
            jax.ShapeDtypeStruct((T // tm, GW, tm), BF16),
            jax.ShapeDtypeStruct((T // tm, 1, GW), F32),
            jax.ShapeDtypeStruct((T, rw), F32),
        ],
        scratch_shapes=[pltpu.VMEM((1, rw), F32)],
        compiler_params=_cparams("arbitrary"),
        name="in_projection",
    )(x2, norm_g.reshape(1, D), wq, wk, wvt, wr, mu_p)


def _moba_kernel(q_ref, k_ref, vt_ref, kmean_ref, bpast_ref, bdiag_ref, g_ref,
                 o_ref, selb_ref, ot_ref, *, nb, nbp, slopes):
    i = pl.program_id(1)
    blk = MOBA_BLOCK
    n_iota = lax.broadcasted_iota(jnp.int32, (nbp, blk), 0)
    valid = n_iota < i
    for h in range(N_HEADS):
        hs = slice(h * HEAD_DIM, (h + 1) * HEAD_DIM)
        qh = q_ref[:, hs]
        km_hi, km_lo = _split2(kmean_ref[0, :, hs])
        gate = _dot_nt(km_hi, qh) + _dot_nt(km_lo, qh)
        gate = jnp.where(valid, gate, NEG_INF)
        rank = jnp.zeros((nbp, blk), jnp.int32)
        for m in range(nb):
            gm = gate[m:m + 1, :]
            tie = (n_iota > m).astype(jnp.int32)
            rank = rank + jnp.where(gm > gate, 1, jnp.where(gm == gate, tie, 0))
        selb_ref[...] = jnp.where(rank < MOBA_TOPK, jnp.where(valid, 0.0, NEG_INF), NEG_INF)

        kd = k_ref[0, pl.ds(pl.multiple_of(i * blk, blk), blk), hs]
        st = _dot_nt(kd, qh) + bdiag_ref[h]
        m0 = jnp.max(st, axis=0, keepdims=True)
        p = jnp.exp(st - m0)
        l0 = jnp.sum(p, axis=0, keepdims=True)
        acc0 = _dot(vt_ref[0, i, hs, :], p.astype(BF16))

        def body(j, carry, h=h, hs=hs, qh=qh):
            m_run, l_run, acc = carry
            kj = k_ref[0, pl.ds(pl.multiple_of(j * blk, blk), blk), hs]
            off = selb_ref[pl.ds(j, 1), :] - (slopes[h] * blk) * (i - j).astype(F32)
            st = _dot_nt(kj, qh) + bpast_ref[h] + off
            m_new = jnp.maximum(m_run, jnp.max(st, axis=0, keepdims=True))
            alpha = jnp.exp(m_run - m_new)
            p = jnp.exp(st - m_new)
            l_new = alpha * l_run + jnp.sum(p, axis=0, keepdims=True)
            acc = alpha * acc + _dot(vt_ref[0, j, hs, :], p.astype(BF16))
            return m_new, l_new, acc

        _, l_fin, acc = lax.fori_loop(0, i, body, (m0, l0, acc0))
        o = acc / l_fin
        o = o * lax.rsqrt(jnp.mean(o * o, axis=0, keepdims=True) + NORM_EPS)
        ot_ref[hs, :] = o
    o_ref[...] = (ot_ref[...].T * g_ref[...]).astype(o_ref.dtype)


def _moba_bias_tables():
    slopes = [2.0 ** (-8.0 * (h + 1) / N_HEADS) for h in range(N_HEADS)]
    key = jnp.arange(MOBA_BLOCK, dtype=F32)[:, None]
    qry = jnp.arange(MOBA_BLOCK, dtype=F32)[None, :]
    dist = qry - key
    sl = jnp.asarray(slopes, F32)[:, None, None]
    past = -sl * dist[None]
    diag = jnp.where(dist[None] >= 0, past, NEG_INF)
    return slopes, past, diag


def _moba_attention(q, k, vt, kmean, attn_norm_g, batch, seq):
    GW = GROUP_WIDTH
    blk = MOBA_BLOCK
    nb = seq // blk
    nbp = -(-nb // 16) * 16
    slopes, bpast, bdiag = _moba_bias_tables()
    k3 = k.reshape(batch, seq, GW)
    vt4 = vt.reshape(batch, nb, GW, blk)
    km = jnp.pad(kmean.reshape(batch, nb, GW), ((0, 0), (0, nbp - nb), (0, 0)))
    return pl.pallas_call(
        functools.partial(_moba_kernel, nb=nb, nbp=nbp, slopes=slopes),
        grid=(batch, nb),
        in_specs=[
            pl.BlockSpec((blk, GW), lambda b, i: (b * nb + i, 0)),
            pl.BlockSpec((1, seq, GW), lambda b, i: (b, 0, 0)),
            pl.BlockSpec((1, nb, GW, blk), lambda b, i: (b, 0, 0, 0)),
            pl.BlockSpec((1, nbp, GW), lambda b, i: (b, 0, 0)),
            pl.BlockSpec((N_HEADS, blk, blk), lambda b, i: (0, 0, 0)),
            pl.BlockSpec((N_HEADS, blk, blk), lambda b, i: (0, 0, 0)),
            pl.BlockSpec((1, GW), lambda b, i: (0, 0)),
        ],
        out_specs=pl.BlockSpec((blk, GW), lambda b, i: (b * nb + i, 0)),
        out_shape=jax.ShapeDtypeStruct((batch * seq, GW), BF16),
        scratch_shapes=[pltpu.VMEM((nbp, blk), F32), pltpu.VMEM((GW, blk), F32)],
        compiler_params=_cparams("arbitrary", "arbitrary"),
        name="moba_attention",
    )(q, k3, vt4, km, bpast, bdiag, attn_norm_g.reshape(1, GW))


def _mm(a, b, dims=((1,), (0,)), passes=3):
    dn = (dims, ((), ()))
    dg = lambda u, v: lax.dot_general(u, v, dn, preferred_element_type=F32)
    if passes == 1:
        return dg(a.astype(BF16), b.astype(BF16))
    a_hi, a_lo = _split2(a)
    b_hi, b_lo = _split2(b)
    return dg(a_hi, b_hi) + dg(a_hi, b_lo) + dg(a_lo, b_hi)


_NT = ((1,), (1,))
_TN = ((0,), (0,))


def _mm_left(a, b_bf16):
    a_hi, a_lo = _split2(a)
    return _dot(a_hi, b_bf16) + _dot(a_lo, b_bf16)


def _rwkv_kernel(xs_ref, w2_ref, a2_ref, g2_ref, vec_ref, tril_ref, hsum_ref,
                 o_ref, state_ref, cum_ref, lw_ref, r_ref, kk_ref, b_ref, k2_ref, v_ref, oc_ref):
    GW = GROUP_WIDTH
    L = RWKV_CHUNK
    N = HEAD_DIM
    tm = xs_ref.shape[0]
    w0, a0, k_k, k_a, r_k, ln_g, ln_b = [vec_ref[n:n + 1, :] for n in range(7)]

    @pl.when(pl.program_id(1) == 0)
    def _():
        state_ref[...] = jnp.zeros_like(state_ref)

    r = xs_ref[:, 0:GW]
    k = xs_ref[:, GW:2 * GW]
    v = xs_ref[:, 2 * GW:3 * GW]
    lo = xs_ref[:, 3 * GW:]
    hsum = hsum_ref[...]
    logw = -RWKV_DECAY_SCALE * jax.nn.sigmoid(w0 + _mm(jnp.tanh(lo), w2_ref[...]))
    a = jax.nn.sigmoid(a0 + _mm(lo, a2_ref[...]))
    gate = _mm(jax.nn.sigmoid(lo), g2_ref[...])
    kk = k * k_k
    kk = kk * lax.rsqrt(jnp.maximum(_mm_left(kk * kk, hsum), 1e-24))
    k2 = k * (1.0 + (a - 1.0) * k_a)
    bonus = _mm_left(r * k2 * r_k, hsum) * v
    lw_hi, lw_lo = _split2(logw)
    cum_ref[...] = _dot(tril_ref[...], lw_hi) + _dot(tril_ref[...], lw_lo)
    lw_ref[...] = logw
    r_ref[...] = r
    kk_ref[...] = kk
    b_ref[...] = kk * a
    k2_ref[...] = k2
    v_ref[...] = v

    row = lax.broadcasted_iota(jnp.int32, (L, L), 0)
    col = lax.broadcasted_iota(jnp.int32, (L, L), 1)
    strict = row > col
    incl = row >= col
    eye = row == col

    def chunk(c, _):
        rows = pl.ds(pl.multiple_of(c * L, L), L)
        cum = cum_ref[rows, :]
        cum_end = cum[L - 1:L, :]
        w_incl = jnp.exp(cum)
        w_excl = jnp.exp(cum - lw_ref[rows, :])
        w_inv = jnp.exp(-cum)
        e_end = jnp.exp(cum_end - cum)
        w_end = jnp.exp(cum_end)
        rhat = r_ref[rows, :] * w_incl
        kkhat = kk_ref[rows, :] * w_excl
        btil = b_ref[rows, :] * w_inv
        ktil = k2_ref[rows, :] * w_inv
        bbar = b_ref[rows, :] * e_end
        kbar = k2_ref[rows, :] * e_end
        vc = v_ref[rows, :]
        for h in range(N_HEADS):
            hs = slice(h * N, (h + 1) * N)
            lhs = jnp.concatenate([kkhat[:, hs], rhat[:, hs]], axis=0)
            rhs = jnp.concatenate([btil[:, hs], ktil[:, hs]], axis=0)
            ma = _mm(lhs, rhs, _NT)
            m_bk = jnp.where(strict, ma[:L, :L], 0.0)
            m_kk = jnp.where(strict, ma[:L, L:], 0.0)
            a_br = jnp.where(incl, ma[L:, :L], 0.0)
            a_kr = jnp.where(incl, ma[L:, L:], 0.0)
            vh = vc[:, hs]
            y = jnp.concatenate([kkhat[:, hs], _mm(m_kk, vh)], axis=1)
            y = y - _mm(m_bk, y)
            pw = m_bk
            for _ in range(int(math.log2(L)) - 1):
                pw = _mm(pw, pw)
                y = y + _mm(pw, y)
            xtb = _mm(y, bbar[:, hs], _TN)
            p_mat = jnp.where(eye, w_end[:, hs], 0.0) - xtb[:N]
            q_mat = _mm(vh, kbar[:, hs], _TN) - xtb[N:]
            ax = _mm(a_br, y)
            r2 = rhat[:, hs] - ax[:, :N]
            o0 = _mm(a_kr, vh) - ax[:, N:]
            s0 = state_ref[h]
            oc_ref[rows, hs] = _mm(r2, s0, _NT) + o0
            state_ref[h] = _mm(s0, p_mat) + q_mat
        return 0

    lax.fori_loop(0, tm // L, chunk, 0)

    o = oc_ref[...]
    mean = _mm_left(o, hsum) * (1.0 / N)
    d = o - mean
    var = _mm_left(d * d, hsum) * (1.0 / N)
    o = d * lax.rsqrt(var + RWKV_GN_EPS) * ln_g + ln_b
    o_ref[...] = ((o + bonus) * gate).astype(o_ref.dtype)


def _rwkv_mix(xs, w0, w2, a0, a2, g2, k_k, k_a, r_k, ln_g, ln_b, batch, seq):
    T, rw = xs.shape
    GW = GROUP_WIDTH
    L = RWKV_CHUNK
    tm = 256
    nl = rw - 3 * GW
    d0, d1, d2 = RWKV_LORA
    w2p = jnp.zeros((nl, GW), F32).at[0:d0].set(w2)
    a2p = jnp.zeros((nl, GW), F32).at[d0:d0 + d1].set(a2)
    g2p = jnp.zeros((nl, GW), F32).at[d0 + d1:d0 + d1 + d2].set(g2)
    vecs = jnp.stack([w0, a0, k_k, k_a, r_k.reshape(GW), ln_g, ln_b, jnp.zeros((GW,), F32)])
    t_idx = jnp.arange(tm)
    tril = ((t_idx[:, None] >= t_idx[None, :]) & (t_idx[:, None] // L == t_idx[None, :] // L)).astype(BF16)
    c_idx = jnp.arange(GW) // HEAD_DIM
    hsum = (c_idx[:, None] == c_idx[None, :]).astype(BF16)
    tps = seq // tm
    const = lambda b, i: (0, 0)
    scr = lambda: pltpu.VMEM((tm, GW), F32)
    return pl.pallas_call(
        _rwkv_kernel,
        grid=(batch, tps),
        in_specs=[
            pl.BlockSpec((tm, rw), lambda b, i: (b * tps + i, 0)),
            pl.BlockSpec((nl, GW), const),
            pl.BlockSpec((nl, GW), const),
            pl.BlockSpec((nl, GW), const),
            pl.BlockSpec((8, GW), const),
            pl.BlockSpec((tm, tm), const),
            pl.BlockSpec((GW, GW), const),
        ],
        out_specs=pl.BlockSpec((tm, GW), lambda b, i: (b * tps + i, 0)),
        out_shape=jax.ShapeDtypeStruct((T, GW), BF16),
        scratch_shapes=[pltpu.VMEM((N_HEADS, HEAD_DIM, HEAD_DIM), F32)] + [scr() for _ in range(8)],
        compiler_params=_cparams("arbitrary", "arbitrary"),
        name="rwkv7_mix",
    )(xs, w2p, a2p, g2p, vecs, tril, hsum)


EXPERT_LANES = 128


def _outproj_router_kernel(oa_ref, or_ref, x_ref, wa_ref, wr_ref, g_ref, wrt_ref, brt_ref,
                           h_ref, xn_ref, idx_ref, gate_ref, cnt_ref):
    h = x_ref[...] + _dot(oa_ref[...], wa_ref[...]) + _dot(or_ref[...], wr_ref[...])
    h_ref[...] = h
    xn = h * lax.rsqrt(jnp.mean(h * h, axis=-1, keepdims=True) + NORM_EPS) * g_ref[...]
    xn_ref[...] = xn
    logits = _mm(xn, wrt_ref[...]) + brt_ref[...]
    tm = logits.shape[0]
    lane = lax.broadcasted_iota(jnp.int32, logits.shape, 1)
    lane4 = lax.broadcasted_iota(jnp.int32, (tm, TOP_K), 1)
    idx_out = jnp.zeros((tm, TOP_K), jnp.int32)
    val_out = jnp.zeros((tm, TOP_K), F32)
    member = jnp.zeros(logits.shape, F32)
    top = None
    denom = jnp.zeros((tm, 1), F32)
    for kk in range(TOP_K):
        mx = jnp.max(logits, axis=-1, keepdims=True)
        idx = jnp.min(jnp.where(logits == mx, lane, EXPERT_LANES), axis=-1, keepdims=True)
        hit = lane == idx
        member = jnp.where(hit, 1.0, member)
        logits = jnp.where(hit, -jnp.inf, logits)
        top = mx if top is None else top
        e = jnp.exp(mx - top)
        denom = denom + e
        idx_out = jnp.where(lane4 == kk, idx, idx_out)
        val_out = jnp.where(lane4 == kk, e, val_out)
    idx_ref[...] = idx_out
    gate_ref[...] = val_out / denom

    @pl.when(pl.program_id(0) == 0)
    def _():
        cnt_ref[...] = jnp.zeros_like(cnt_ref)

    cnt_ref[...] += jnp.sum(member, axis=0, keepdims=True)


def _outproj_router(o_att, o_rwkv, x2, w_out, norm_g, w_router, b_router):
    T, D = x2.shape
    GW = GROUP_WIDTH
    tm = 256
    E = w_router.shape[1]
    wa = w_out[:GW].astype(BF16)
    wr = w_out[GW:].astype(BF16)
    wrt = jnp.pad(w_router, ((0, 0), (0, EXPERT_LANES - E)))
    brt = jnp.pad(b_router, (0, EXPERT_LANES - E), constant_values=NEG_INF).reshape(1, EXPERT_LANES)
    const = lambda i: (0, 0)
    tile = lambda w: pl.BlockSpec((tm, w), lambda i: (i, 0))
    return pl.pallas_call(
        _outproj_router_kernel,
        grid=(T // tm,),
        in_specs=[tile(GW), tile(GW), tile(D),
                  pl.BlockSpec((GW, D), const), pl.BlockSpec((GW, D), const),
                  pl.BlockSpec((1, D), const), pl.BlockSpec((D, EXPERT_LANES), const),
                  pl.BlockSpec((1, EXPERT_LANES), const)],
        out_specs=[tile(D), tile(D), tile(TOP_K), tile(TOP_K), pl.BlockSpec((1, EXPERT_LANES), const)],
        out_shape=[jax.ShapeDtypeStruct((T, D), F32), jax.ShapeDtypeStruct((T, D), F32),
                   jax.ShapeDtypeStruct((T, TOP_K), jnp.int32), jax.ShapeDtypeStruct((T, TOP_K), F32),
                   jax.ShapeDtypeStruct((1, EXPERT_LANES), F32)],
        compiler_params=_cparams("arbitrary"),
        name="outproj_router",
    )(o_att, o_rwkv, x2, wa, wr, norm_g.reshape(1, D), wrt, brt)


def _route_kernel(idx_ref, start_ref, tril_ref, dest_ref, carry_ref):
    @pl.when(pl.program_id(0) == 0)
    def _():
        carry_ref[...] = jnp.zeros_like(carry_ref)

    idx = idx_ref[...]
    tm = idx.shape[0]
    lane = lax.broadcasted_iota(jnp.int32, (tm, EXPERT_LANES), 1)
    lane4 = lax.broadcasted_iota(jnp.int32, (tm, TOP_K), 1)
    hits = [lane == idx[:, kk:kk + 1] for kk in range(TOP_K)]
    member = sum(jnp.where(hh, 1.0, 0.0) for hh in hits)
    before = _dot(tril_ref[...], member.astype(BF16)) + carry_ref[...]
    base = before + start_ref[...]
    dest = jnp.zeros((tm, TOP_K), jnp.int32)
    for kk in range(TOP_K):
        d = jnp.sum(jnp.where(hits[kk], base, 0.0), axis=-1, keepdims=True)
        dest = jnp.where(lane4 == kk, d.astype(jnp.int32), dest)
    dest_ref[...] = dest
    carry_ref[...] += jnp.sum(member, axis=0, keepdims=True)


def _route(idx, pad_start):
    T = idx.shape[0]
    tm = 512
    t_idx = jnp.arange(tm)
    tril = (t_idx[:, None] > t_idx[None, :]).astype(BF16)
    const = lambda i: (0, 0)
    return pl.pallas_call(
        _route_kernel,
        grid=(T // tm,),
        in_specs=[pl.BlockSpec((tm, TOP_K), lambda i: (i, 0)),
                  pl.BlockSpec((1, EXPERT_LANES), const),
                  pl.BlockSpec((tm, tm), const)],
        out_specs=pl.BlockSpec((tm, TOP_K), lambda i: (i, 0)),
        out_shape=jax.ShapeDtypeStruct((T, TOP_K), jnp.int32),
        scratch_shapes=[pltpu.VMEM((1, EXPERT_LANES), F32)],
        compiler_params=_cparams("arbitrary"),
        name="moe_route",
    )(idx, pad_start, tril)


def _scatter_kernel(dest_ref, x_ref, slots_in_ref, slots_ref, sem):
    del slots_in_ref
    tm = x_ref.shape[0]

    def row_copy(t, kk):
        d = dest_ref[t * TOP_K + kk]
        return pltpu.make_async_copy(x_ref.at[pl.ds(t, 1), :], slots_ref.at[pl.ds(d, 1), :], sem)

    def issue(t, _):
        for kk in range(TOP_K):
            row_copy(t, kk).start()
        return 0

    def drain(t, _):
        for kk in range(TOP_K):
            row_copy(t, kk).wait()
        return 0

    lax.fori_loop(0, tm, issue, 0)
    lax.fori_loop(0, tm, drain, 0)


def _scatter_rows(xn, dest_flat, n_slots):
    T, D = xn.shape
    tm = 256
    slots0 = jnp.zeros((n_slots, D), xn.dtype)
    return pl.pallas_call(
        _scatter_kernel,
        grid=(T // tm,),
        in_specs=[pl.BlockSpec((tm * TOP_K,), lambda i: (i,), memory_space=pltpu.SMEM),
                  pl.BlockSpec((tm, D), lambda i: (i, 0)),
                  pl.BlockSpec(memory_space=pl.ANY)],
        out_specs=pl.BlockSpec(memory_space=pl.ANY),
        out_shape=jax.ShapeDtypeStruct((n_slots, D), xn.dtype),
        scratch_shapes=[pltpu.SemaphoreType.DMA(())],
        input_output_aliases={2: 0},
        compiler_params=_cparams("arbitrary"),
        name="moe_scatter",
    )(dest_flat, xn, slots0)


def _expert_kernel(be_ref, nu_ref, x_ref, wg_ref, bg_ref, wu_ref, bu_ref, wd_ref, bd_ref, y_ref):
    del be_ref
    used = pl.program_id(0) < nu_ref[0]

    @pl.when(jnp.logical_not(used))
    def _():
        y_ref[...] = jnp.zeros_like(y_ref)

    @pl.when(used)
    def _():
        xb = x_ref[...].astype(BF16)
        gate = jnp.minimum(_dot(xb, wg_ref[0]) + bg_ref[0], SWIGLU_LIMIT)
        up = jnp.clip(_dot(xb, wu_ref[0]) + bu_ref[0], -SWIGLU_LIMIT, SWIGLU_LIMIT)
        glu = gate * jax.nn.sigmoid(SWIGLU_ALPHA * gate)
        y_ref[...] = _dot(((up + 1.0) * glu).astype(BF16), wd_ref[0]) + bd_ref[0]


def _expert_ffn(slots, block_expert, n_used, w_gate, b_gate, w_up, b_up, w_down, b_down):
    P, D = slots.shape
    E, _, F = w_gate.shape
    bm = MOE_BLOCK
    row = lambda i, be, nu: (jnp.minimum(i, nu[0] - 1), 0)
    wsel = lambda i, be, nu: (be[i], 0, 0)
    grid_spec = pltpu.PrefetchScalarGridSpec(
        num_scalar_prefetch=2,
        grid=(P // bm,),
        in_specs=[pl.BlockSpec((bm, D), row),
                  pl.BlockSpec((1, D, F), wsel), pl.BlockSpec((1, 1, F), wsel),
                  pl.BlockSpec((1, D, F), wsel), pl.BlockSpec((1, 1, F), wsel),
                  pl.BlockSpec((1, F, D), wsel), pl.BlockSpec((1, 1, D), wsel)],
        out_specs=pl.BlockSpec((bm, D), lambda i, be, nu: (i, 0)),
    )
    return pl.pallas_call(
        _expert_kernel,
        grid_spec=grid_spec,
        out_shape=jax.ShapeDtypeStruct((P, D), F32),
        compiler_params=_cparams("arbitrary"),
        name="moe_experts",
    )(block_expert, n_used, slots, w_gate.astype(BF16), b_gate.reshape(E, 1, F), w_up.astype(BF16),
      b_up.reshape(E, 1, F), w_down.astype(BF16), b_down.reshape(E, 1, D))


def _combine_kernel(dest_ref, h_ref, gate_ref, g_ref, y_ref, o_ref, buf_ref, sem):
    tm = h_ref.shape[0]

    def row_copy(t, kk):
        d = dest_ref[t * TOP_K + kk]
        return pltpu.make_async_copy(y_ref.at[pl.ds(d, 1), :], buf_ref.at[kk, pl.ds(t, 1), :], sem)

    def issue(t, _):
        for kk in range(TOP_K):
            row_copy(t, kk).start()
        return 0

    def drain(t, _):
        for kk in range(TOP_K):
            row_copy(t, kk).wait()
        return 0

    lax.fori_loop(0, tm, issue, 0)
    lax.fori_loop(0, tm, drain, 0)
    gates = gate_ref[...]
    h = h_ref[...]
    for kk in range(TOP_K):
        h = h + gates[:, kk:kk + 1] * buf_ref[kk]
    o_ref[...] = h * lax.rsqrt(jnp.mean(h * h, axis=-1, keepdims=True) + NORM_EPS) * g_ref[...]


def _combine(h1, gates, dest_flat, y_slots, norm_g):
    T, D = h1.shape
    tm = 256
    return pl.pallas_call(
        _combine_kernel,
        grid=(T // tm,),
        in_specs=[pl.BlockSpec((tm * TOP_K,), lambda i: (i,), memory_space=pltpu.SMEM),
                  pl.BlockSpec((tm, D), lambda i: (i, 0)),
                  pl.BlockSpec((tm, TOP_K), lambda i: (i, 0)),
                  pl.BlockSpec((1, D), lambda i: (0, 0)),
                  pl.BlockSpec(memory_space=pl.ANY)],
        out_specs=pl.BlockSpec((tm, D), lambda i: (i, 0)),
        out_shape=jax.ShapeDtypeStruct((T, D), F32),
        scratch_shapes=[pltpu.VMEM((TOP_K, tm, D), F32), pltpu.SemaphoreType.DMA(())],
        compiler_params=_cparams("arbitrary"),
        name="moe_combine",
    )(dest_flat, h1, gates, norm_g.reshape(1, D), y_slots)


def _moe(h1, xn2, idx, gates, counts, w_gate, b_gate, w_up, b_up, w_down, b_down, norm_final_g):
    T, D = h1.shape
    E = w_gate.shape[0]
    n_blocks = T * TOP_K // MOE_BLOCK + E
    cnt = counts[0].astype(jnp.int32)
    padded = (cnt + MOE_BLOCK - 1) // MOE_BLOCK * MOE_BLOCK
    pad_end = jnp.cumsum(padded)
    pad_start = (pad_end - padded).astype(F32).reshape(1, EXPERT_LANES)
    block_expert = jnp.minimum(
        jnp.searchsorted(pad_end[:E], jnp.arange(n_blocks) * MOE_BLOCK, side='right'), E - 1).astype(jnp.int32)
    n_used = (pad_end[E - 1] // MOE_BLOCK).astype(jnp.int32).reshape(1)
    dest = _route(idx, pad_start).reshape(T * TOP_K)
    slots = _scatter_rows(xn2, dest, n_blocks * MOE_BLOCK)
    y_slots = _expert_ffn(slots, block_expert, n_used, w_gate, b_gate, w_up, b_up, w_down, b_down)
    return _combine(h1, gates, dest, y_slots, norm_final_g)


def kernel(x, norm_mix_g, w_in, attn_norm_g, rwkv_mu, rwkv_w0, rwkv_w2, rwkv_a0, rwkv_a2,
           rwkv_g2, rwkv_k_k, rwkv_k_a, rwkv_r_k, rwkv_ln_g, rwkv_ln_b, w_out, norm_ffn_g,
           w_router, b_router, moe_w_gate, moe_b_gate, moe_w_up, moe_b_up, moe_w_down,
           moe_b_down, norm_final_g):
    B, S, D = x.shape
    x2 = x.reshape(B * S, D)
    q, k, vt, km, xs = _in_projection(x2, norm_mix_g[0], w_in[0], rwkv_mu[0], B, S)
    o_att = _moba_attention(q, k, vt, km, attn_norm_g[0], B, S)
    o_rwkv = _rwkv_mix(xs, rwkv_w0[0], rwkv_w2[0], rwkv_a0[0], rwkv_a2[0], rwkv_g2[0], rwkv_k_k[0],
                       rwkv_k_a[0], rwkv_r_k[0], rwkv_ln_g[0], rwkv_ln_b[0], B, S)
    h1, xn2, idx, gates, counts = _outproj_router(o_att, o_rwkv, x2, w_out[0], norm_ffn_g[0],
                                                  w_router[0], b_router[0])
    out = _moe(h1, xn2, idx, gates, counts, moe_w_gate[0], moe_b_gate[0], moe_w_up[0], moe_b_up[0],
               moe_w_down[0], moe_b_down[0], norm_final_g)
    return out.reshape(B, S, D)
```

```python
import functools
import math

import jax
import jax.numpy as jnp
from jax import lax
from jax.experimental import pallas as pl
from jax.experimental.pallas import tpu as pltpu

F32 = jnp.float32
BF16 = jnp.bfloat16

HEAD_DIM = 64
N_HEADS = 8
GROUP_WIDTH = N_HEADS * HEAD_DIM
MOBA_BLOCK = 256
MOBA_TOPK = 3
N_EXPERTS = 32
TOP_K = 4
MOE_BLOCK = 256
SWIGLU_LIMIT = 7.0
SWIGLU_ALPHA = 1.702
NORM_EPS = 1e-6
NEG_INF = -1e30
RWKV_DECAY_SCALE = math.exp(-0.5)
RWKV_GN_EPS = HEAD_DIM * 1e-5
RWKV_LORA = (32, 32, 96)
RWKV_LORA_PAD = 256
RWKV_CHUNK = 64
RWKV_CHUNKS_PER_STEP = 4

VMEM_LIMIT = 48 * 1024 * 1024


def _cparams(*sem):
    return pltpu.CompilerParams(dimension_semantics=sem, vmem_limit_bytes=VMEM_LIMIT)


def _dot(a, b):
    return jnp.dot(a, b, preferred_element_type=F32)


def _dot_nt(a, b):
    return lax.dot_general(a, b, (((1,), (1,)), ((), ())), preferred_element_type=F32)


def _dot_tn(a, b):
    return lax.dot_general(a, b, (((0,), (0,)), ((), ())), preferred_element_type=F32)


def _split2(x):
    hi = x.astype(BF16)
    lo = (x - hi.astype(F32)).astype(BF16)
    return hi, lo


def _inproj_kernel(x_ref, g_ref, wq_ref, wk_ref, wvt_ref, wr_ref, mu_ref,
                   q_ref, k_ref, vt_ref, kmean_ref, xs_ref, carry_ref, *, tiles_per_seq):
    i = pl.program_id(0)
    x = x_ref[...]
    xn = x * lax.rsqrt(jnp.mean(x * x, axis=-1, keepdims=True) + NORM_EPS) * g_ref[...]
    xb = xn.astype(BF16)
    q_ref[...] = (_dot(xb, wq_ref[...]) * (HEAD_DIM ** -0.5)).astype(BF16)
    k = _dot(xb, wk_ref[...])
    k_ref[...] = k.astype(BF16)
    kmean_ref[0] = jnp.mean(k, axis=0, keepdims=True)
    vt_ref[0] = _dot_nt(wvt_ref[...], xb).astype(BF16)
    pr = _dot(xb, wr_ref[...])

    @pl.when(i % tiles_per_seq == 0)
    def _():
        carry_ref[...] = jnp.zeros_like(carry_ref)

    tm = pr.shape[0]
    rolled = pltpu.roll(pr, 1, 0)
    row = lax.broadcasted_iota(jnp.int32, pr.shape, 0)
    prev = jnp.where(row == 0, carry_ref[...], rolled)
    carry_ref[...] = pr[tm - 1:tm, :]
    xs_ref[...] = pr + (prev - pr) * mu_ref[...]


def _in_projection(x2, norm_g, w_in, mu, batch, seq):
    T, D = x2.shape
    tm = MOBA_BLOCK
    nb = seq // tm
    GW = GROUP_WIDTH
    wq = w_in[:, :GW].astype(BF16)
    wk = w_in[:, GW:2 * GW].astype(BF16)
    wvt = w_in[:, 2 * GW:3 * GW].T.astype(BF16)
    rw = 3 * GW + RWKV_LORA_PAD
    n_r = w_in.shape[1] - 3 * GW
    wr = jnp.pad(w_in[:, 3 * GW:], ((0, 0), (0, rw - n_r))).astype(BF16)
    mu_p = jnp.pad(mu, (0, rw - n_r)).reshape(1, rw)
    const = lambda i: (0, 0)
    return pl.pallas_call(
        functools.partial(_inproj_kernel, tiles_per_seq=nb),
        grid=(T // tm,),
        in_specs=[
            pl.BlockSpec((tm, D), lambda i: (i, 0)),
            pl.BlockSpec((1, D), const),
            pl.BlockSpec((D, GW), const),
            pl.BlockSpec((D, GW), const),
            pl.BlockSpec((GW, D), const),
            pl.BlockSpec((D, rw), const),
            pl.BlockSpec((1, rw), const),
        ],
        out_specs=[
            pl.BlockSpec((tm, GW), lambda i: (i, 0)),
            pl.BlockSpec((tm, GW), lambda i: (i, 0)),
            pl.BlockSpec((1, GW, tm), lambda i: (i, 0, 0)),
            pl.BlockSpec((1, 1, GW), lambda i: (i, 0, 0)),
            pl.BlockSpec((tm, rw), lambda i: (i, 0)),
        ],
        out_shape=[
            jax.ShapeDtypeStruct((T, GW), BF16),
            jax.ShapeDtypeStruct((T, GW), BF16),
            jax.ShapeDtypeStruct((T // tm, GW, tm), BF16),
            jax.ShapeDtypeStruct((T // tm, 1, GW), F32),
            jax.ShapeDtypeStruct((T, rw), F32),
        ],
        scratch_shapes=[pltpu.VMEM((1, rw), F32)],
        compiler_params=_cparams("arbitrary"),
        name="in_projection",
    )(x2, norm_g.reshape(1, D), wq, wk, wvt, wr, mu_p)


def _moba_kernel(q_ref, k_ref, vt_ref, kmean_ref, bpast_ref, bdiag_ref, g_ref,
                 o_ref, qm_ref, selb_ref, m_ref, l_ref, acc_ref, *, nb, nbp, slopes):
    i = pl.program_id(1)
    blk = MOBA_BLOCK
    heads = range(N_HEADS)
    pair = 2 * HEAD_DIM
    tile = [slice((h // 2) * pair, (h // 2 + 1) * pair) for h in heads]
    vrow = [slice(h * HEAD_DIM, (h + 1) * HEAD_DIM) for h in heads]
    n_iota = lax.broadcasted_iota(jnp.int32, (nbp, blk), 0)
    valid = n_iota < i
    half = lax.broadcasted_iota(jnp.int32, (blk, pair), 1) // HEAD_DIM

    for h in heads:
        qm_ref[h] = jnp.where(half == h % 2, q_ref[:, tile[h]], jnp.zeros((), BF16))

    gates = []
    for h in heads:
        km_hi, km_lo = _split2(kmean_ref[0, :, tile[h]])
        gates.append(jnp.where(valid, _dot_nt(km_hi, qm_ref[h]) + _dot_nt(km_lo, qm_ref[h]), NEG_INF))
    for h in heads:
        rank = jnp.zeros((nbp, blk), jnp.int32)
        for m in range(nb):
            gm = gates[h][m:m + 1, :]
            tie = (n_iota > m).astype(jnp.int32)
            rank = rank + jnp.where(gm > gates[h], 1, jnp.where(gm == gates[h], tie, 0))
        selb_ref[h] = jnp.where(rank < MOBA_TOPK, jnp.where(valid, 0.0, NEG_INF), NEG_INF)

    own = pl.ds(pl.multiple_of(i * blk, blk), blk)
    st = [_dot_nt(k_ref[0, own, tile[h]], qm_ref[h]) + bdiag_ref[h] for h in heads]
    ps = []
    for h in heads:
        m0 = jnp.max(st[h], axis=0, keepdims=True)
        p = jnp.exp(st[h] - m0)
        m_ref[h] = m0
        l_ref[h] = jnp.sum(p, axis=0, keepdims=True)
        ps.append(p.astype(BF16))
    for h in heads:
        acc_ref[vrow[h], :] = _dot(vt_ref[0, i, vrow[h], :], ps[h])

    def body(j, _):
        rows = pl.ds(pl.multiple_of(j * blk, blk), blk)
        dist = (i - j).astype(F32) * blk
        st = [_dot_nt(k_ref[0, rows, tile[h]], qm_ref[h]) for h in heads]
        ps, alphas = [], []
        for h in heads:
            off = selb_ref[h, pl.ds(j, 1), :] - slopes[h] * dist
            s = st[h] + bpast_ref[h] + off
            m_old = m_ref[h]
            m_new = jnp.maximum(m_old, jnp.max(s, axis=0, keepdims=True))
            alpha = jnp.exp(m_old - m_new)
            p = jnp.exp(s - m_new)
            m_ref[h] = m_new
            l_ref[h] = alpha * l_ref[h] + jnp.sum(p, axis=0, keepdims=True)
            ps.append(p.astype(BF16))
            alphas.append(alpha)
        for h in heads:
            acc_ref[vrow[h], :] = alphas[h] * acc_ref[vrow[h], :] + _dot(vt_ref[0, j, vrow[h], :], ps[h])
        return 0

    lax.fori_loop(0, i, body, 0)

    for h in heads:
        o = acc_ref[vrow[h], :] / l_ref[h]
        acc_ref[vrow[h], :] = o * lax.rsqrt(jnp.mean(o * o, axis=0, keepdims=True) + NORM_EPS)
    o_ref[...] = (acc_ref[...].T * g_ref[...]).astype(o_ref.dtype)


def _moba_bias_tables():
    slopes = [2.0 ** (-8.0 * (h + 1) / N_HEADS) for h in range(N_HEADS)]
    key = jnp.arange(MOBA_BLOCK, dtype=F32)[:, None]
    qry = jnp.arange(MOBA_BLOCK, dtype=F32)[None, :]
    dist = qry - key
    sl = jnp.asarray(slopes, F32)[:, None, None]
    past = -sl * dist[None]
    diag = jnp.where(dist[None] >= 0, past, NEG_INF)
    return slopes, past, diag


def _moba_attention(q, k, vt, kmean, attn_norm_g, batch, seq):
    GW = GROUP_WIDTH
    blk = MOBA_BLOCK
    nb = seq // blk
    nbp = -(-nb // 16) * 16
    slopes, bpast, bdiag = _moba_bias_tables()
    k3 = k.reshape(batch, seq, GW)
    vt4 = vt.reshape(batch, nb, GW, blk)
    km = jnp.pad(kmean.reshape(batch, nb, GW), ((0, 0), (0, nbp - nb), (0, 0)))
    return pl.pallas_call(
        functools.partial(_moba_kernel, nb=nb, nbp=nbp, slopes=slopes),
        grid=(batch, nb),
        in_specs=[
            pl.BlockSpec((blk, GW), lambda b, i: (b * nb + i, 0)),
            pl.BlockSpec((1, seq, GW), lambda b, i: (b, 0, 0)),
            pl.BlockSpec((1, nb, GW, blk), lambda b, i: (b, 0, 0, 0)),
            pl.BlockSpec((1, nbp, GW), lambda b, i: (b, 0, 0)),
            pl.BlockSpec((N_HEADS, blk, blk), lambda b, i: (0, 0, 0)),
            pl.BlockSpec((N_HEADS, blk, blk), lambda b, i: (0, 0, 0)),
            pl.BlockSpec((1, GW), lambda b, i: (0, 0)),
        ],
        out_specs=pl.BlockSpec((blk, GW), lambda b, i: (b * nb + i, 0)),
        out_shape=jax.ShapeDtypeStruct((batch * seq, GW), BF16),
        scratch_shapes=[pltpu.VMEM((N_HEADS, blk, 2 * HEAD_DIM), BF16),
                        pltpu.VMEM((N_HEADS, nbp, blk), F32),
                        pltpu.VMEM((N_HEADS, 1, blk), F32),
                        pltpu.VMEM((N_HEADS, 1, blk), F32),
                        pltpu.VMEM((GW, blk), F32)],
        compiler_params=_cparams("arbitrary", "arbitrary"),
        name="moba_attention",
    )(q, k3, vt4, km, bpast, bdiag, attn_norm_g.reshape(1, GW))


def _mm(a, b, dims=((1,), (0,)), passes=3):
    dn = (dims, ((), ()))
    dg = lambda u, v: lax.dot_general(u, v, dn, preferred_element_type=F32)
    if passes == 1:
        return dg(a.astype(BF16), b.astype(BF16))
    a_hi, a_lo = _split2(a)
    b_hi, b_lo = _split2(b)
    return dg(a_hi, b_hi) + dg(a_hi, b_lo) + dg(a_lo, b_hi)


_NT = ((1,), (1,))
_TN = ((0,), (0,))
RWKV_PASSES = {"scores": 1, "inverse": 1, "outer": 1, "state": 1}


def _head_sum(x, pair_ones):
    x_hi, x_lo = _split2(x)
    w = pair_ones.shape[0]
    tiles = [_dot(x_hi[:, t:t + w], pair_ones) + _dot(x_lo[:, t:t + w], pair_ones)
             for t in range(0, x.shape[1], w)]
    return jnp.concatenate(tiles, axis=1)


def _rwkv_kernel(xs_ref, w2_ref, a2_ref, g2_ref, vec_ref, tril_ref, hsum_ref,
                 o_ref, state_ref, cum_ref, lw_ref, r_ref, kk_ref, b_ref, k2_ref, v_ref, oc_ref):
    GW = GROUP_WIDTH
    L = RWKV_CHUNK
    N = HEAD_DIM
    tm = xs_ref.shape[0]
    w0, a0, k_k, k_a, r_k, ln_g, ln_b = [vec_ref[n:n + 1, :] for n in range(7)]

    @pl.when(pl.program_id(1) == 0)
    def _():
        state_ref[...] = jnp.zeros_like(state_ref)

    r = xs_ref[:, 0:GW]
    k = xs_ref[:, GW:2 * GW]
    v = xs_ref[:, 2 * GW:3 * GW]
    lo = xs_ref[:, 3 * GW:]
    hsum = hsum_ref[...]
    logw = -RWKV_DECAY_SCALE * jax.nn.sigmoid(w0 + _dot(jnp.tanh(lo).astype(BF16), w2_ref[...]))
    a = jax.nn.sigmoid(a0 + _dot(lo.astype(BF16), a2_ref[...]))
    gate = _dot(jax.nn.sigmoid(lo).astype(BF16), g2_ref[...])
    kk = k * k_k
    kk = kk * lax.rsqrt(jnp.maximum(_head_sum(kk * kk, hsum), 1e-24))
    k2 = k * (1.0 + (a - 1.0) * k_a)
    bonus = _head_sum(r * k2 * r_k, hsum) * v
    lw_hi, lw_lo = _split2(logw)
    cum_ref[...] = _dot(tril_ref[...], lw_hi) + _dot(tril_ref[...], lw_lo)
    lw_ref[...] = logw
    r_ref[...] = r
    kk_ref[...] = kk
    b_ref[...] = kk * a
    k2_ref[...] = k2
    v_ref[...] = v

    row = lax.broadcasted_iota(jnp.int32, (L, L), 0)
    col = lax.broadcasted_iota(jnp.int32, (L, L), 1)
    strict = row > col
    incl = row >= col
    eye = row == col

    U = RWKV_CHUNKS_PER_STEP
    hsl = [slice(h * N, (h + 1) * N) for h in range(N_HEADS)]
    units = [(u, h) for u in range(U) for h in range(N_HEADS)]

    def step(c, _):
        pre = []
        for u in range(U):
            rows = pl.ds(pl.multiple_of((c * U + u) * L, L), L)
            cum = cum_ref[rows, :]
            cum_end = cum[L - 1:L, :]
            w_incl = jnp.exp(cum)
            w_excl = jnp.exp(cum - lw_ref[rows, :])
            w_inv = jnp.exp(-cum)
            e_end = jnp.exp(cum_end - cum)
            pre.append(dict(
                rows=rows, w_end=jnp.exp(cum_end),
                rhat=r_ref[rows, :] * w_incl, kkhat=kk_ref[rows, :] * w_excl,
                btil=b_ref[rows, :] * w_inv, ktil=k2_ref[rows, :] * w_inv,
                bbar=b_ref[rows, :] * e_end, kbar=k2_ref[rows, :] * e_end, v=v_ref[rows, :]))
        sl = lambda name, u, h: pre[u][name][:, hsl[h]]
        ma = [_mm(jnp.concatenate([sl("kkhat", u, h), sl("rhat", u, h)], axis=0),
                  jnp.concatenate([sl("btil", u, h), sl("ktil", u, h)], axis=0), _NT, RWKV_PASSES["scores"])
              for u, h in units]
        m_bk = [jnp.where(strict, m[:L, :L], 0.0) for m in ma]
        m_kk = [jnp.where(strict, m[:L, L:], 0.0) for m in ma]
        a_br = [jnp.where(incl, m[L:, :L], 0.0) for m in ma]
        a_kr = [jnp.where(incl, m[L:, L:], 0.0) for m in ma]
        n_u = range(len(units))
        vh = [sl("v", u, h) for u, h in units]
        pi = RWKV_PASSES["inverse"]
        y = [jnp.concatenate([sl("kkhat", *units[n]), _mm(m_kk[n], vh[n], passes=pi)], axis=1) for n in n_u]
        y = [y[n] - _mm(m_bk[n], y[n], passes=pi) for n in n_u]
        pw = m_bk
        for _ in range(int(math.log2(L)) - 1):
            pw = [_mm(pw[n], pw[n], passes=pi) for n in n_u]
            y = [y[n] + _mm(pw[n], y[n], passes=pi) for n in n_u]
        po = RWKV_PASSES["outer"]
        xtb = [_mm(y[n], sl("bbar", *units[n]), _TN, po) for n in n_u]
        vtk = [_mm(vh[n], sl("kbar", *units[n]), _TN, po) for n in n_u]
        ax = [_mm(a_br[n], y[n], passes=po) for n in n_u]
        akv = [_mm(a_kr[n], vh[n], passes=po) for n in n_u]
        ps = RWKV_PASSES["state"]
        state = [state_ref[h] for h in range(N_HEADS)]
        for n, (u, h) in enumerate(units):
            r2 = sl("rhat", u, h) - ax[n][:, :N]
            oc_ref[pre[u]["rows"], hsl[h]] = _mm(r2, state[h], _NT, ps) + akv[n] - ax[n][:, N:]
            p_mat = jnp.where(eye, pre[u]["w_end"][:, hsl[h]], 0.0) - xtb[n][:N]
            state[h] = _mm(state[h], p_mat, passes=ps) + vtk[n] - xtb[n][N:]
        for h in range(N_HEADS):
            state_ref[h] = state[h]
        return 0

    lax.fori_loop(0, tm // (L * U), step, 0)

    o = oc_ref[...]
    mean = _head_sum(o, hsum) * (1.0 / N)
    d = o - mean
    var = _head_sum(d * d, hsum) * (1.0 / N)
    o = d * lax.rsqrt(var + RWKV_GN_EPS) * ln_g + ln_b
    o_ref[...] = ((o + bonus) * gate).astype(o_ref.dtype)


def _rwkv_mix(xs, w0, w2, a0, a2, g2, k_k, k_a, r_k, ln_g, ln_b, batch, seq):
    T, rw = xs.shape
    GW = GROUP_WIDTH
    L = RWKV_CHUNK
    tm = 256
    nl = rw - 3 * GW
    d0, d1, d2 = RWKV_LORA
    w2p = jnp.pad(w2, ((0, nl - d0), (0, 0))).astype(BF16)
    a2p = jnp.pad(a2, ((d0, nl - d0 - d1), (0, 0))).astype(BF16)
    g2p = jnp.pad(g2, ((d0 + d1, nl - d0 - d1 - d2), (0, 0))).astype(BF16)
    vecs = jnp.stack([w0, a0, k_k, k_a, r_k.reshape(GW), ln_g, ln_b, jnp.zeros((GW,), F32)])
    t_idx = jnp.arange(tm)
    tril = ((t_idx[:, None] >= t_idx[None, :]) & (t_idx[:, None] // L == t_idx[None, :] // L)).astype(BF16)
    c_idx = jnp.arange(2 * HEAD_DIM) // HEAD_DIM
    hsum = (c_idx[:, None] == c_idx[None, :]).astype(BF16)
    tps = seq // tm
    const = lambda b, i: (0, 0)
    scr = lambda: pltpu.VMEM((tm, GW), F32)
    return pl.pallas_call(
        _rwkv_kernel,
        grid=(batch, tps),
        in_specs=[
            pl.BlockSpec((tm, rw), lambda b, i: (b * tps + i, 0)),
            pl.BlockSpec((nl, GW), const),
            pl.BlockSpec((nl, GW), const),
            pl.BlockSpec((nl, GW), const),
            pl.BlockSpec((8, GW), const),
            pl.BlockSpec((tm, tm), const),
            pl.BlockSpec((2 * HEAD_DIM, 2 * HEAD_DIM), const),
        ],
        out_specs=pl.BlockSpec((tm, GW), lambda b, i: (b * tps + i, 0)),
        out_shape=jax.ShapeDtypeStruct((T, GW), BF16),
        scratch_shapes=[pltpu.VMEM((N_HEADS, HEAD_DIM, HEAD_DIM), F32)] + [scr() for _ in range(8)],
        compiler_params=_cparams("arbitrary", "arbitrary"),
        name="rwkv7_mix",
    )(xs, w2p, a2p, g2p, vecs, tril, hsum)


EXPERT_LANES = 128


def _outproj_router_kernel(oa_ref, or_ref, x_ref, wa_ref, wr_ref, g_ref, wrt_ref, brt_ref,
                           h_ref, xn_ref, idx_ref, gate_ref, cnt_ref):
    h = x_ref[...] + _dot(oa_ref[...], wa_ref[...]) + _dot(or_ref[...], wr_ref[...])
    h_ref[...] = h
    xn = h * lax.rsqrt(jnp.mean(h * h, axis=-1, keepdims=True) + NORM_EPS) * g_ref[...]
    xn_ref[...] = xn
    logits = _mm(xn, wrt_ref[...]) + brt_ref[...]
    tm = logits.shape[0]
    lane = lax.broadcasted_iota(jnp.int32, logits.shape, 1)
    lane4 = lax.broadcasted_iota(jnp.int32, (tm, TOP_K), 1)
    idx_out = jnp.zeros((tm, TOP_K), jnp.int32)
    val_out = jnp.zeros((tm, TOP_K), F32)
    member = jnp.zeros(logits.shape, F32)
    top = None
    denom = jnp.zeros((tm, 1), F32)
    for kk in range(TOP_K):
        mx = jnp.max(logits, axis=-1, keepdims=True)
        idx = jnp.min(jnp.where(logits == mx, lane, EXPERT_LANES), axis=-1, keepdims=True)
        hit = lane == idx
        member = jnp.where(hit, 1.0, member)
        logits = jnp.where(hit, -jnp.inf, logits)
        top = mx if top is None else top
        e = jnp.exp(mx - top)
        denom = denom + e
        idx_out = jnp.where(lane4 == kk, idx, idx_out)
        val_out = jnp.where(lane4 == kk, e, val_out)
    idx_ref[...] = idx_out
    gate_ref[...] = val_out / denom

    @pl.when(pl.program_id(0) == 0)
    def _():
        cnt_ref[...] = jnp.zeros_like(cnt_ref)

    cnt_ref[...] += jnp.sum(member, axis=0, keepdims=True)


def _outproj_router(o_att, o_rwkv, x2, w_out, norm_g, w_router, b_router):
    T, D = x2.shape
    GW = GROUP_WIDTH
    tm = 256
    E = w_router.shape[1]
    wa = w_out[:GW].astype(BF16)
    wr = w_out[GW:].astype(BF16)
    wrt = jnp.pad(w_router, ((0, 0), (0, EXPERT_LANES - E)))
    brt = jnp.pad(b_router, (0, EXPERT_LANES - E), constant_values=NEG_INF).reshape(1, EXPERT_LANES)
    const = lambda i: (0, 0)
    tile = lambda w: pl.BlockSpec((tm, w), lambda i: (i, 0))
    return pl.pallas_call(
        _outproj_router_kernel,
        grid=(T // tm,),
        in_specs=[tile(GW), tile(GW), tile(D),
                  pl.BlockSpec((GW, D), const), pl.BlockSpec((GW, D), const),
                  pl.BlockSpec((1, D), const), pl.BlockSpec((D, EXPERT_LANES), const),
                  pl.BlockSpec((1, EXPERT_LANES), const)],
        out_specs=[tile(D), tile(D), tile(TOP_K), tile(TOP_K), pl.BlockSpec((1, EXPERT_LANES), const)],
        out_shape=[jax.ShapeDtypeStruct((T, D), F32), jax.ShapeDtypeStruct((T, D), F32),
                   jax.ShapeDtypeStruct((T, TOP_K), jnp.int32), jax.ShapeDtypeStruct((T, TOP_K), F32),
                   jax.ShapeDtypeStruct((1, EXPERT_LANES), F32)],
        compiler_params=_cparams("arbitrary"),
        name="outproj_router",
    )(o_att, o_rwkv, x2, wa, wr, norm_g.reshape(1, D), wrt, brt)


def _route_kernel(idx_ref, start_ref, tril_ref, dest_ref, carry_ref):
    @pl.when(pl.program_id(0) == 0)
    def _():
        carry_ref[...] = jnp.zeros_like(carry_ref)

    idx = idx_ref[...]
    tm = idx.shape[0]
    lane = lax.broadcasted_iota(jnp.int32, (tm, EXPERT_LANES), 1)
    lane4 = lax.broadcasted_iota(jnp.int32, (tm, TOP_K), 1)
    hits = [lane == idx[:, kk:kk + 1] for kk in range(TOP_K)]
    member = sum(jnp.where(hh, 1.0, 0.0) for hh in hits)
    before = _dot(tril_ref[...], member.astype(BF16)) + carry_ref[...]
    base = before + start_ref[...]
    dest = jnp.zeros((tm, TOP_K), jnp.int32)
    for kk in range(TOP_K):
        d = jnp.sum(jnp.where(hits[kk], base, 0.0), axis=-1, keepdims=True)
        dest = jnp.where(lane4 == kk, d.astype(jnp.int32), dest)
    dest_ref[...] = dest
    carry_ref[...] += jnp.sum(member, axis=0, keepdims=True)


def _route(idx, pad_start):
    T = idx.shape[0]
    tm = 512
    t_idx = jnp.arange(tm)
    tril = (t_idx[:, None] > t_idx[None, :]).astype(BF16)
    const = lambda i: (0, 0)
    return pl.pallas_call(
        _route_kernel,
        grid=(T // tm,),
        in_specs=[pl.BlockSpec((tm, TOP_K), lambda i: (i, 0)),
                  pl.BlockSpec((1, EXPERT_LANES), const),
                  pl.BlockSpec((tm, tm), const)],
        out_specs=pl.BlockSpec((tm, TOP_K), lambda i: (i, 0)),
        out_shape=jax.ShapeDtypeStruct((T, TOP_K), jnp.int32),
        scratch_shapes=[pltpu.VMEM((1, EXPERT_LANES), F32)],
        compiler_params=_cparams("arbitrary"),
        name="moe_route",
    )(idx, pad_start, tril)


def _scatter_kernel(dest_ref, x_ref, slots_in_ref, slots_ref, sem):
    del slots_in_ref
    tm = x_ref.shape[0]

    def row_copy(t, kk):
        d = dest_ref[t * TOP_K + kk]
        return pltpu.make_async_copy(x_ref.at[pl.ds(t, 1), :], slots_ref.at[pl.ds(d, 1), :], sem)

    def issue(t, _):
        for kk in range(TOP_K):
            row_copy(t, kk).start(priority=kk % 2)
        return 0

    def drain(t, _):
        for kk in range(TOP_K):
            row_copy(t, kk).wait()
        return 0

    lax.fori_loop(0, tm, issue, 0)
    lax.fori_loop(0, tm, drain, 0)


def _scatter_rows(xn, dest_flat, n_slots):
    T, D = xn.shape
    tm = 256
    slots0 = jnp.zeros((n_slots, D), xn.dtype)
    return pl.pallas_call(
        _scatter_kernel,
        grid=(T // tm,),
        in_specs=[pl.BlockSpec((tm * TOP_K,), lambda i: (i,), memory_space=pltpu.SMEM),
                  pl.BlockSpec((tm, D), lambda i: (i, 0)),
                  pl.BlockSpec(memory_space=pl.ANY)],
        out_specs=pl.BlockSpec(memory_space=pl.ANY),
        out_shape=jax.ShapeDtypeStruct((n_slots, D), xn.dtype),
        scratch_shapes=[pltpu.SemaphoreType.DMA(())],
        input_output_aliases={2: 0},
        compiler_params=_cparams("arbitrary"),
        name="moe_scatter",
    )(dest_flat, xn, slots0)


def _expert_kernel(be_ref, nu_ref, x_ref, wg_ref, bg_ref, wu_ref, bu_ref, wd_ref, bd_ref, y_ref):
    del be_ref
    used = pl.program_id(0) < nu_ref[0]

    @pl.when(jnp.logical_not(used))
    def _():
        y_ref[...] = jnp.zeros_like(y_ref)

    @pl.when(used)
    def _():
        xb = x_ref[...].astype(BF16)
        gate = jnp.minimum(_dot(xb, wg_ref[0]) + bg_ref[0], SWIGLU_LIMIT)
        up = jnp.clip(_dot(xb, wu_ref[0]) + bu_ref[0], -SWIGLU_LIMIT, SWIGLU_LIMIT)
        glu = gate * jax.nn.sigmoid(SWIGLU_ALPHA * gate)
        y_ref[...] = _dot(((up + 1.0) * glu).astype(BF16), wd_ref[0]) + bd_ref[0]


def _expert_ffn(slots, block_expert, n_used, w_gate, b_gate, w_up, b_up, w_down, b_down):
    P, D = slots.shape
    E, _, F = w_gate.shape
    bm = MOE_BLOCK
    row = lambda i, be, nu: (jnp.minimum(i, nu[0] - 1), 0)
    wsel = lambda i, be, nu: (be[i], 0, 0)
    grid_spec = pltpu.PrefetchScalarGridSpec(
        num_scalar_prefetch=2,
        grid=(P // bm,),
        in_specs=[pl.BlockSpec((bm, D), row),
                  pl.BlockSpec((1, D, F), wsel), pl.BlockSpec((1, 1, F), wsel),
                  pl.BlockSpec((1, D, F), wsel), pl.BlockSpec((1, 1, F), wsel),
                  pl.BlockSpec((1, F, D), wsel), pl.BlockSpec((1, 1, D), wsel)],
        out_specs=pl.BlockSpec((bm, D), lambda i, be, nu: (i, 0)),
    )
    return pl.pallas_call(
        _expert_kernel,
        grid_spec=grid_spec,
        out_shape=jax.ShapeDtypeStruct((P, D), F32),
        compiler_params=_cparams("arbitrary"),
        name="moe_experts",
    )(block_expert, n_used, slots, w_gate.astype(BF16), b_gate.reshape(E, 1, F), w_up.astype(BF16),
      b_up.reshape(E, 1, F), w_down.astype(BF16), b_down.reshape(E, 1, D))


def _combine_kernel(dest_ref, h_ref, gate_ref, g_ref, y_ref, o_ref, buf_ref, sem):
    tm = h_ref.shape[0]

    def row_copy(t, kk):
        d = dest_ref[t * TOP_K + kk]
        return pltpu.make_async_copy(y_ref.at[pl.ds(d, 1), :], buf_ref.at[kk, pl.ds(t, 1), :], sem)

    def issue(t, _):
        for kk in range(TOP_K):
            row_copy(t, kk).start(priority=kk % 2)
        return 0

    def drain(t, _):
        for kk in range(TOP_K):
            row_copy(t, kk).wait()
        return 0

    lax.fori_loop(0, tm, issue, 0)
    lax.fori_loop(0, tm, drain, 0)
    gates = gate_ref[...]
    h = h_ref[...]
    for kk in range(TOP_K):
        h = h + gates[:, kk:kk + 1] * buf_ref[kk]
    o_ref[...] = h * lax.rsqrt(jnp.mean(h * h, axis=-1, keepdims=True) + NORM_EPS) * g_ref[...]


def _combine(h1, gates, dest_flat, y_slots, norm_g):
    T, D = h1.shape
    tm = 256
    return pl.pallas_call(
        _combine_kernel,
        grid=(T // tm,),
        in_specs=[pl.BlockSpec((tm * TOP_K,), lambda i: (i,), memory_space=pltpu.SMEM),
                  pl.BlockSpec((tm, D), lambda i: (i, 0)),
                  pl.BlockSpec((tm, TOP_K), lambda i: (i, 0)),
                  pl.BlockSpec((1, D), lambda i: (0, 0)),
                  pl.BlockSpec(memory_space=pl.ANY)],
        out_specs=pl.BlockSpec((tm, D), lambda i: (i, 0)),
        out_shape=jax.ShapeDtypeStruct((T, D), F32),
        scratch_shapes=[pltpu.VMEM((TOP_K, tm, D), F32), pltpu.SemaphoreType.DMA(())],
        compiler_params=_cparams("arbitrary"),
        name="moe_combine",
    )(dest_flat, h1, gates, norm_g.reshape(1, D), y_slots)


def _moe(h1, xn2, idx, gates, counts, w_gate, b_gate, w_up, b_up, w_down, b_down, norm_final_g):
    T, D = h1.shape
    E = w_gate.shape[0]
    n_blocks = T * TOP_K // MOE_BLOCK + E
    cnt = counts[0].astype(jnp.int32)
    padded = (cnt + MOE_BLOCK - 1) // MOE_BLOCK * MOE_BLOCK
    pad_end = jnp.cumsum(padded)
    pad_start = (pad_end - padded).astype(F32).reshape(1, EXPERT_LANES)
    block_first = jnp.arange(n_blocks, dtype=jnp.int32) * MOE_BLOCK
    block_expert = jnp.minimum(
        jnp.sum((pad_end[None, :E] <= block_first[:, None]).astype(jnp.int32), axis=1), E - 1)
    n_used = (pad_end[E - 1] // MOE_BLOCK).astype(jnp.int32).reshape(1)
    dest = _route(idx, pad_start).reshape(T * TOP_K)
    slots = _scatter_rows(xn2, dest, n_blocks * MOE_BLOCK)
    y_slots = _expert_ffn(slots, block_expert, n_used, w_gate, b_gate, w_up, b_up, w_down, b_down)
    return _combine(h1, gates, dest, y_slots, norm_final_g)


def kernel(x, norm_mix_g, w_in, attn_norm_g, rwkv_mu, rwkv_w0, rwkv_w2, rwkv_a0, rwkv_a2,
           rwkv_g2, rwkv_k_k, rwkv_k_a, rwkv_r_k, rwkv_ln_g, rwkv_ln_b, w_out, norm_ffn_g,
           w_router, b_router, moe_w_gate, moe_b_gate, moe_w_up, moe_b_up, moe_w_down,
           moe_b_down, norm_final_g):
    B, S, D = x.shape
    x2 = x.reshape(B * S, D)
    q, k, vt, km, xs = _in_projection(x2, norm_mix_g[0], w_in[0], rwkv_mu[0], B, S)
    o_att = _moba_attention(q, k, vt, km, attn_norm_g[0], B, S)
    o_rwkv = _rwkv_mix(xs, rwkv_w0[0], rwkv_w2[0], rwkv_a0[0], rwkv_a2[0], rwkv_g2[0], rwkv_k_k[0],
                       rwkv_k_a[0], rwkv_r_k[0], rwkv_ln_g[0], rwkv_ln_b[0], B, S)
    h1, xn2, idx, gates, counts = _outproj_router(o_att, o_rwkv, x2, w_out[0], norm_ffn_g[0],
                                                  w_router[0], b_router[0])
    out = _moe(h1, xn2, idx, gates, counts, moe_w_gate[0], moe_b_gate[0], moe_w_up[0], moe_b_up[0],
               moe_w_down[0], moe_b_down[0], norm_final_g)
    return out.reshape(B, S, D)
```

```python
import functools
import math

import jax
import jax.numpy as jnp
from jax import lax
from jax.experimental import pallas as pl
from jax.experimental.pallas import tpu as pltpu

F32 = jnp.float32
BF16 = jnp.bfloat16

HEAD_DIM = 64
N_HEADS = 8
GROUP_WIDTH = N_HEADS * HEAD_DIM
ATTN_TILE_WIDTH = N_HEADS * 2 * HEAD_DIM
MOBA_BLOCK = 256
MOBA_TOPK = 3
N_EXPERTS = 32
TOP_K = 4
MOE_BLOCK = 256
SWIGLU_LIMIT = 7.0
SWIGLU_ALPHA = 1.702
NORM_EPS = 1e-6
NEG_INF = -1e30
RWKV_DECAY_SCALE = math.exp(-0.5)
RWKV_GN_EPS = HEAD_DIM * 1e-5
RWKV_LORA = (32, 32, 96)
RWKV_LORA_PAD = 256
RWKV_CHUNK = 64
RWKV_CHUNKS_PER_STEP = 4

VMEM_LIMIT = 48 * 1024 * 1024


def _cparams(*sem):
    return pltpu.CompilerParams(dimension_semantics=sem, vmem_limit_bytes=VMEM_LIMIT)


def _dot(a, b):
    return jnp.dot(a, b, preferred_element_type=F32)


def _dot_nt(a, b):
    return lax.dot_general(a, b, (((1,), (1,)), ((), ())), preferred_element_type=F32)


def _dot_tn(a, b):
    return lax.dot_general(a, b, (((0,), (0,)), ((), ())), preferred_element_type=F32)


def _split2(x):
    hi = x.astype(BF16)
    lo = (x - hi.astype(F32)).astype(BF16)
    return hi, lo


def _inproj_kernel(x_ref, g_ref, wq_ref, wk_ref, wvt_ref, wr_ref, mu_ref, qaug_ref, kaug_ref,
                   q_ref, k_ref, vt_ref, kmean_ref, xs_ref, carry_ref, *, tiles_per_seq):
    i = pl.program_id(0)
    x = x_ref[...]
    xn = x * lax.rsqrt(jnp.mean(x * x, axis=-1, keepdims=True) + NORM_EPS) * g_ref[...]
    xb = xn.astype(BF16)
    q_ref[...] = (_dot(xb, wq_ref[...]) * (HEAD_DIM ** -0.5) + qaug_ref[...]).astype(BF16)
    k = _dot(xb, wk_ref[...])
    k_ref[...] = (k + kaug_ref[...]).astype(BF16)
    kmean_ref[0] = jnp.mean(k, axis=0, keepdims=True)
    vt_ref[0] = _dot_nt(wvt_ref[...], xb).astype(BF16)
    pr = _dot(xb, wr_ref[...])

    @pl.when(i % tiles_per_seq == 0)
    def _():
        carry_ref[...] = jnp.zeros_like(carry_ref)

    tm = pr.shape[0]
    rolled = pltpu.roll(pr, 1, 0)
    row = lax.broadcasted_iota(jnp.int32, pr.shape, 0)
    prev = jnp.where(row == 0, carry_ref[...], rolled)
    carry_ref[...] = pr[tm - 1:tm, :]
    xs_ref[...] = pr + (prev - pr) * mu_ref[...]


def _in_projection(x2, norm_g, w_in, mu, batch, seq):
    T, D = x2.shape
    tm = MOBA_BLOCK
    nb = seq // tm
    GW = GROUP_WIDTH
    AW = ATTN_TILE_WIDTH

    def spread(w):
        w = w.reshape(D, N_HEADS, HEAD_DIM)
        return jnp.pad(w, ((0, 0), (0, 0), (0, HEAD_DIM))).reshape(D, AW).astype(BF16)

    wq = spread(w_in[:, :GW])
    wk = spread(w_in[:, GW:2 * GW])
    pos = jnp.arange(tm, dtype=F32)[:, None, None]
    slope = jnp.asarray(_alibi_slopes(), F32)[None, :, None]
    lane = jnp.arange(2 * HEAD_DIM)[None, None, :]
    qaug = jnp.where(lane == HEAD_DIM, -slope * pos, jnp.where(lane == HEAD_DIM + 1, 1.0, 0.0)).reshape(tm, AW)
    kaug = jnp.where(lane == HEAD_DIM, 1.0, jnp.where(lane == HEAD_DIM + 1, slope * pos, 0.0)).reshape(tm, AW)
    wvt = w_in[:, 2 * GW:3 * GW].T.astype(BF16)
    rw = 3 * GW + RWKV_LORA_PAD
    n_r = w_in.shape[1] - 3 * GW
    wr = jnp.pad(w_in[:, 3 * GW:], ((0, 0), (0, rw - n_r))).astype(BF16)
    mu_p = jnp.pad(mu, (0, rw - n_r)).reshape(1, rw)
    const = lambda i: (0, 0)
    return pl.pallas_call(
        functools.partial(_inproj_kernel, tiles_per_seq=nb),
        grid=(T // tm,),
        in_specs=[
            pl.BlockSpec((tm, D), lambda i: (i, 0)),
            pl.BlockSpec((1, D), const),
            pl.BlockSpec((D, AW), const),
            pl.BlockSpec((D, AW), const),
            pl.BlockSpec((GW, D), const),
            pl.BlockSpec((D, rw), const),
            pl.BlockSpec((1, rw), const),
            pl.BlockSpec((tm, AW), const),
            pl.BlockSpec((tm, AW), const),
        ],
        out_specs=[
            pl.BlockSpec((tm, AW), lambda i: (i, 0)),
            pl.BlockSpec((tm, AW), lambda i: (i, 0)),
            pl.BlockSpec((1, GW, tm), lambda i: (i, 0, 0)),
            pl.BlockSpec((1, 1, AW), lambda i: (i, 0, 0)),
            pl.BlockSpec((tm, rw), lambda i: (i, 0)),
        ],
        out_shape=[
            jax.ShapeDtypeStruct((T, AW), BF16),
            jax.ShapeDtypeStruct((T, AW), BF16),
            jax.ShapeDtypeStruct((T // tm, GW, tm), BF16),
            jax.ShapeDtypeStruct((T // tm, 1, AW), F32),
            jax.ShapeDtypeStruct((T, rw), F32),
        ],
        scratch_shapes=[pltpu.VMEM((1, rw), F32)],
        compiler_params=_cparams("arbitrary"),
        name="in_projection",
    )(x2, norm_g.reshape(1, D), wq, wk, wvt, wr, mu_p, qaug, kaug)


def _moba_kernel(q_ref, k_ref, vt_ref, kmean_ref, g_ref,
                 o_ref, selb_ref, m_ref, l_ref, acc_ref, *, nb, nbp, slopes):
    i = pl.program_id(1)
    blk = MOBA_BLOCK
    heads = range(N_HEADS)
    tile = [slice(h * 2 * HEAD_DIM, (h + 1) * 2 * HEAD_DIM) for h in heads]
    vrow = [slice(h * HEAD_DIM, (h + 1) * HEAD_DIM) for h in heads]
    n_iota = lax.broadcasted_iota(jnp.int32, (nbp, blk), 0)
    valid = n_iota < i

    gates = []
    for h in heads:
        km_hi, km_lo = _split2(kmean_ref[0, :, tile[h]])
        qh = q_ref[:, tile[h]]
        gates.append(jnp.where(valid, _dot_nt(km_hi, qh) + _dot_nt(km_lo, qh), NEG_INF))
    for h in heads:
        rank = jnp.zeros((nbp, blk), jnp.int32)
        for m in range(nb):
            gm = gates[h][m:m + 1, :]
            tie = (n_iota > m).astype(jnp.int32)
            rank = rank + jnp.where(gm > gates[h], 1, jnp.where(gm == gates[h], tie, 0))
        selb_ref[h] = jnp.where(rank < MOBA_TOPK, jnp.where(valid, 0.0, NEG_INF), NEG_INF)

    own = pl.ds(pl.multiple_of(i * blk, blk), blk)
    future = (lax.broadcasted_iota(jnp.int32, (blk, blk), 0) >
              lax.broadcasted_iota(jnp.int32, (blk, blk), 1))
    st = [jnp.where(future, NEG_INF, _dot_nt(k_ref[0, own, tile[h]], q_ref[:, tile[h]])) for h in heads]
    ps = []
    for h in heads:
        m0 = jnp.max(st[h], axis=0, keepdims=True)
        p = jnp.exp(st[h] - m0)
        m_ref[h] = m0
        l_ref[h] = jnp.sum(p, axis=0, keepdims=True)
        ps.append(p.astype(BF16))
    for h in heads:
        acc_ref[vrow[h], :] = _dot(vt_ref[0, i, vrow[h], :], ps[h])

    def body(j, _):
        rows = pl.ds(pl.multiple_of(j * blk, blk), blk)
        dist = (i - j).astype(F32) * blk
        st = [_dot_nt(k_ref[0, rows, tile[h]], q_ref[:, tile[h]]) for h in heads]
        ps, alphas = [], []
        for h in heads:
            off = selb_ref[h, pl.ds(j, 1), :] - slopes[h] * dist
            m_old = m_ref[h]
            m_new = jnp.maximum(m_old, jnp.max(st[h], axis=0, keepdims=True) + off)
            alpha = jnp.exp(m_old - m_new)
            p = jnp.exp(st[h] - (m_new - off))
            m_ref[h] = m_new
            l_ref[h] = alpha * l_ref[h] + jnp.sum(p, axis=0, keepdims=True)
            ps.append(p.astype(BF16))
            alphas.append(alpha)
        for h in heads:
            acc_ref[vrow[h], :] = alphas[h] * acc_ref[vrow[h], :] + _dot(vt_ref[0, j, vrow[h], :], ps[h])
        return 0

    lax.fori_loop(0, i, body, 0)

    for h in heads:
        o = acc_ref[vrow[h], :] / l_ref[h]
        acc_ref[vrow[h], :] = o * lax.rsqrt(jnp.mean(o * o, axis=0, keepdims=True) + NORM_EPS)
    o_ref[...] = (acc_ref[...].T * g_ref[...]).astype(o_ref.dtype)


def _alibi_slopes():
    return [2.0 ** (-8.0 * (h + 1) / N_HEADS) for h in range(N_HEADS)]


def _moba_attention(q, k, vt, kmean, attn_norm_g, batch, seq):
    GW = GROUP_WIDTH
    AW = ATTN_TILE_WIDTH
    blk = MOBA_BLOCK
    nb = seq // blk
    nbp = -(-nb // 16) * 16
    k3 = k.reshape(batch, seq, AW)
    vt4 = vt.reshape(batch, nb, GW, blk)
    km = jnp.pad(kmean.reshape(batch, nb, AW), ((0, 0), (0, nbp - nb), (0, 0)))
    return pl.pallas_call(
        functools.partial(_moba_kernel, nb=nb, nbp=nbp, slopes=_alibi_slopes()),
        grid=(batch, nb),
        in_specs=[
            pl.BlockSpec((blk, AW), lambda b, i: (b * nb + i, 0)),
            pl.BlockSpec((1, seq, AW), lambda b, i: (b, 0, 0)),
            pl.BlockSpec((1, nb, GW, blk), lambda b, i: (b, 0, 0, 0)),
            pl.BlockSpec((1, nbp, AW), lambda b, i: (b, 0, 0)),
            pl.BlockSpec((1, GW), lambda b, i: (0, 0)),
        ],
        out_specs=pl.BlockSpec((blk, GW), lambda b, i: (b * nb + i, 0)),
        out_shape=jax.ShapeDtypeStruct((batch * seq, GW), BF16),
        scratch_shapes=[pltpu.VMEM((N_HEADS, nbp, blk), F32),
                        pltpu.VMEM((N_HEADS, 1, blk), F32),
                        pltpu.VMEM((N_HEADS, 1, blk), F32),
                        pltpu.VMEM((GW, blk), F32)],
        compiler_params=_cparams("arbitrary", "arbitrary"),
        name="moba_attention",
    )(q, k3, vt4, km, attn_norm_g.reshape(1, GW))


def _mm(a, b, dims=((1,), (0,)), passes=3):
    dn = (dims, ((), ()))
    dg = lambda u, v: lax.dot_general(u, v, dn, preferred_element_type=F32)
    if passes == 1:
        return dg(a.astype(BF16), b.astype(BF16))
    a_hi, a_lo = _split2(a)
    b_hi, b_lo = _split2(b)
    return dg(a_hi, b_hi) + dg(a_hi, b_lo) + dg(a_lo, b_hi)


_NT = ((1,), (1,))
_TN = ((0,), (0,))
RWKV_PASSES = {"scores": 1, "inverse": 1, "outer": 1, "state": 1}


def _head_sum(x, pair_ones):
    x_hi, x_lo = _split2(x)
    w = pair_ones.shape[0]
    tiles = [_dot(x_hi[:, t:t + w], pair_ones) + _dot(x_lo[:, t:t + w], pair_ones)
             for t in range(0, x.shape[1], w)]
    return jnp.concatenate(tiles, axis=1)


def _rwkv_kernel(xs_ref, w2_ref, a2_ref, g2_ref, vec_ref, tril_ref, hsum_ref,
                 o_ref, state_ref, cum_ref, lw_ref, r_ref, kk_ref, b_ref, k2_ref, v_ref, oc_ref):
    GW = GROUP_WIDTH
    L = RWKV_CHUNK
    N = HEAD_DIM
    tm = xs_ref.shape[0]
    w0, a0, k_k, k_a, r_k, ln_g, ln_b = [vec_ref[n:n + 1, :] for n in range(7)]

    @pl.when(pl.program_id(1) == 0)
    def _():
        state_ref[...] = jnp.zeros_like(state_ref)

    r = xs_ref[:, 0:GW]
    k = xs_ref[:, GW:2 * GW]
    v = xs_ref[:, 2 * GW:3 * GW]
    lo = xs_ref[:, 3 * GW:]
    hsum = hsum_ref[...]
    logw = -RWKV_DECAY_SCALE * jax.nn.sigmoid(w0 + _dot(jnp.tanh(lo).astype(BF16), w2_ref[...]))
    a = jax.nn.sigmoid(a0 + _dot(lo.astype(BF16), a2_ref[...]))
    gate = _dot(jax.nn.sigmoid(lo).astype(BF16), g2_ref[...])
    kk = k * k_k
    kk = kk * lax.rsqrt(jnp.maximum(_head_sum(kk * kk, hsum), 1e-24))
    k2 = k * (1.0 + (a - 1.0) * k_a)
    bonus = _head_sum(r * k2 * r_k, hsum) * v
    lw_hi, lw_lo = _split2(logw)
    cum_ref[...] = _dot(tril_ref[...], lw_hi) + _dot(tril_ref[...], lw_lo)
    lw_ref[...] = logw
    r_ref[...] = r
    kk_ref[...] = kk
    b_ref[...] = kk * a
    k2_ref[...] = k2
    v_ref[...] = v

    row = lax.broadcasted_iota(jnp.int32, (L, L), 0)
    col = lax.broadcasted_iota(jnp.int32, (L, L), 1)
    strict = row > col
    incl = row >= col
    eye = row == col

    U = RWKV_CHUNKS_PER_STEP
    hsl = [slice(h * N, (h + 1) * N) for h in range(N_HEADS)]
    units = [(u, h) for u in range(U) for h in range(N_HEADS)]

    def step(c, _):
        pre = []
        for u in range(U):
            rows = pl.ds(pl.multiple_of((c * U + u) * L, L), L)
            cum = cum_ref[rows, :]
            cum_end = cum[L - 1:L, :]
            w_incl = jnp.exp(cum)
            w_excl = jnp.exp(cum - lw_ref[rows, :])
            w_inv = jnp.exp(-cum)
            e_end = jnp.exp(cum_end - cum)
            pre.append(dict(
                rows=rows, w_end=jnp.exp(cum_end),
                rhat=r_ref[rows, :] * w_incl, kkhat=kk_ref[rows, :] * w_excl,
                btil=b_ref[rows, :] * w_inv, ktil=k2_ref[rows, :] * w_inv,
                bbar=b_ref[rows, :] * e_end, kbar=k2_ref[rows, :] * e_end, v=v_ref[rows, :]))
        sl = lambda name, u, h: pre[u][name][:, hsl[h]]
        ma = [_mm(jnp.concatenate([sl("kkhat", u, h), sl("rhat", u, h)], axis=0),
                  jnp.concatenate([sl("btil", u, h), sl("ktil", u, h)], axis=0), _NT, RWKV_PASSES["scores"])
              for u, h in units]
        m_bk = [jnp.where(strict, m[:L, :L], 0.0) for m in ma]
        m_kk = [jnp.where(strict, m[:L, L:], 0.0) for m in ma]
        a_br = [jnp.where(incl, m[L:, :L], 0.0) for m in ma]
        a_kr = [jnp.where(incl, m[L:, L:], 0.0) for m in ma]
        n_u = range(len(units))
        vh = [sl("v", u, h) for u, h in units]
        pi = RWKV_PASSES["inverse"]
        y = [jnp.concatenate([sl("kkhat", *units[n]), _mm(m_kk[n], vh[n], passes=pi)], axis=1) for n in n_u]
        y = [y[n] - _mm(m_bk[n], y[n], passes=pi) for n in n_u]
        pw = m_bk
        for _ in range(int(math.log2(L)) - 1):
            pw = [_mm(pw[n], pw[n], passes=pi) for n in n_u]
            y = [y[n] + _mm(pw[n], y[n], passes=pi) for n in n_u]
        po = RWKV_PASSES["outer"]
        xtb = [_mm(y[n], sl("bbar", *units[n]), _TN, po) for n in n_u]
        vtk = [_mm(vh[n], sl("kbar", *units[n]), _TN, po) for n in n_u]
        ax = [_mm(a_br[n], y[n], passes=po) for n in n_u]
        akv = [_mm(a_kr[n], vh[n], passes=po) for n in n_u]
        ps = RWKV_PASSES["state"]
        state = [state_ref[h] for h in range(N_HEADS)]
        for n, (u, h) in enumerate(units):
            r2 = sl("rhat", u, h) - ax[n][:, :N]
            oc_ref[pre[u]["rows"], hsl[h]] = _mm(r2, state[h], _NT, ps) + akv[n] - ax[n][:, N:]
            p_mat = jnp.where(eye, pre[u]["w_end"][:, hsl[h]], 0.0) - xtb[n][:N]
            state[h] = _mm(state[h], p_mat, passes=ps) + vtk[n] - xtb[n][N:]
        for h in range(N_HEADS):
            state_ref[h] = state[h]
        return 0

    lax.fori_loop(0, tm // (L * U), step, 0)

    o = oc_ref[...]
    mean = _head_sum(o, hsum) * (1.0 / N)
    d = o - mean
    var = _head_sum(d * d, hsum) * (1.0 / N)
    o = d * lax.rsqrt(var + RWKV_GN_EPS) * ln_g + ln_b
    o_ref[...] = ((o + bonus) * gate).astype(o_ref.dtype)


def _rwkv_mix(xs, w0, w2, a0, a2, g2, k_k, k_a, r_k, ln_g, ln_b, batch, seq):
    T, rw = xs.shape
    GW = GROUP_WIDTH
    L = RWKV_CHUNK
    tm = 256
    nl = rw - 3 * GW
    d0, d1, d2 = RWKV_LORA
    w2p = jnp.pad(w2, ((0, nl - d0), (0, 0))).astype(BF16)
    a2p = jnp.pad(a2, ((d0, nl - d0 - d1), (0, 0))).astype(BF16)
    g2p = jnp.pad(g2, ((d0 + d1, nl - d0 - d1 - d2), (0, 0))).astype(BF16)
    vecs = jnp.stack([w0, a0, k_k, k_a, r_k.reshape(GW), ln_g, ln_b, jnp.zeros((GW,), F32)])
    t_idx = jnp.arange(tm)
    tril = ((t_idx[:, None] >= t_idx[None, :]) & (t_idx[:, None] // L == t_idx[None, :] // L)).astype(BF16)
    c_idx = jnp.arange(2 * HEAD_DIM) // HEAD_DIM
    hsum = (c_idx[:, None] == c_idx[None, :]).astype(BF16)
    tps = seq // tm
    const = lambda b, i: (0, 0)
    scr = lambda: pltpu.VMEM((tm, GW), F32)
    return pl.pallas_call(
        _rwkv_kernel,
        grid=(batch, tps),
        in_specs=[
            pl.BlockSpec((tm, rw), lambda b, i: (b * tps + i, 0)),
            pl.BlockSpec((nl, GW), const),
            pl.BlockSpec((nl, GW), const),
            pl.BlockSpec((nl, GW), const),
            pl.BlockSpec((8, GW), const),
            pl.BlockSpec((tm, tm), const),
            pl.BlockSpec((2 * HEAD_DIM, 2 * HEAD_DIM), const),
        ],
        out_specs=pl.BlockSpec((tm, GW), lambda b, i: (b * tps + i, 0)),
        out_shape=jax.ShapeDtypeStruct((T, GW), BF16),
        scratch_shapes=[pltpu.VMEM((N_HEADS, HEAD_DIM, HEAD_DIM), F32)] + [scr() for _ in range(8)],
        compiler_params=_cparams("arbitrary", "arbitrary"),
        name="rwkv7_mix",
    )(xs, w2p, a2p, g2p, vecs, tril, hsum)


EXPERT_LANES = 128

ROW_SUBLANES = 8
LANES = 128


def _store_row_tiles(ref, x):
    n = x.shape[0]
    for s in range(ROW_SUBLANES):
        ref[pl.ds(s, n, stride=ROW_SUBLANES), :] = x[:, s * LANES:(s + 1) * LANES]


def _load_row_tiles(ref, n):
    return jnp.concatenate([ref[pl.ds(s, n, stride=ROW_SUBLANES), :] for s in range(ROW_SUBLANES)], axis=1)


def _outproj_router_kernel(oa_ref, or_ref, x_ref, wa_ref, wr_ref, g_ref, wrt_ref, brt_ref,
                           h_ref, xn_ref, idx_ref, gate_ref, cnt_ref):
    h = x_ref[...] + _dot(oa_ref[...], wa_ref[...]) + _dot(or_ref[...], wr_ref[...])
    h_ref[...] = h
    xn = h * lax.rsqrt(jnp.mean(h * h, axis=-1, keepdims=True) + NORM_EPS) * g_ref[...]
    _store_row_tiles(xn_ref, xn)
    logits = _mm(xn, wrt_ref[...]) + brt_ref[...]
    tm = logits.shape[0]
    lane = lax.broadcasted_iota(jnp.int32, logits.shape, 1)
    lane4 = lax.broadcasted_iota(jnp.int32, (tm, TOP_K), 1)
    idx_out = jnp.zeros((tm, TOP_K), jnp.int32)
    val_out = jnp.zeros((tm, TOP_K), F32)
    member = jnp.zeros(logits.shape, F32)
    top = None
    denom = jnp.zeros((tm, 1), F32)
    for kk in range(TOP_K):
        mx = jnp.max(logits, axis=-1, keepdims=True)
        idx = jnp.min(jnp.where(logits == mx, lane, EXPERT_LANES), axis=-1, keepdims=True)
        hit = lane == idx
        member = jnp.where(hit, 1.0, member)
        logits = jnp.where(hit, -jnp.inf, logits)
        top = mx if top is None else top
        e = jnp.exp(mx - top)
        denom = denom + e
        idx_out = jnp.where(lane4 == kk, idx, idx_out)
        val_out = jnp.where(lane4 == kk, e, val_out)
    idx_ref[...] = idx_out
    gate_ref[...] = val_out / denom

    @pl.when(pl.program_id(0) == 0)
    def _():
        cnt_ref[...] = jnp.zeros_like(cnt_ref)

    cnt_ref[...] += jnp.sum(member, axis=0, keepdims=True)


def _outproj_router(o_att, o_rwkv, x2, w_out, norm_g, w_router, b_router):
    T, D = x2.shape
    GW = GROUP_WIDTH
    tm = 256
    E = w_router.shape[1]
    wa = w_out[:GW].astype(BF16)
    wr = w_out[GW:].astype(BF16)
    wrt = jnp.pad(w_router, ((0, 0), (0, EXPERT_LANES - E)))
    brt = jnp.pad(b_router, (0, EXPERT_LANES - E), constant_values=NEG_INF).reshape(1, EXPERT_LANES)
    const = lambda i: (0, 0)
    tile = lambda w: pl.BlockSpec((tm, w), lambda i: (i, 0))
    return pl.pallas_call(
        _outproj_router_kernel,
        grid=(T // tm,),
        in_specs=[tile(GW), tile(GW), tile(D),
                  pl.BlockSpec((GW, D), const), pl.BlockSpec((GW, D), const),
                  pl.BlockSpec((1, D), const), pl.BlockSpec((D, EXPERT_LANES), const),
                  pl.BlockSpec((1, EXPERT_LANES), const)],
        out_specs=[tile(D), pl.BlockSpec((tm * ROW_SUBLANES, LANES), lambda i: (i, 0)),
                   tile(TOP_K), tile(TOP_K), pl.BlockSpec((1, EXPERT_LANES), const)],
        out_shape=[jax.ShapeDtypeStruct((T, D), F32), jax.ShapeDtypeStruct((T * ROW_SUBLANES, LANES), F32),
                   jax.ShapeDtypeStruct((T, TOP_K), jnp.int32), jax.ShapeDtypeStruct((T, TOP_K), F32),
                   jax.ShapeDtypeStruct((1, EXPERT_LANES), F32)],
        compiler_params=_cparams("arbitrary"),
        name="outproj_router",
    )(o_att, o_rwkv, x2, wa, wr, norm_g.reshape(1, D), wrt, brt)


def _route_kernel(idx_ref, start_ref, tril_ref, dest_ref, carry_ref):
    @pl.when(pl.program_id(0) == 0)
    def _():
        carry_ref[...] = jnp.zeros_like(carry_ref)

    idx = idx_ref[...]
    tm = idx.shape[0]
    lane = lax.broadcasted_iota(jnp.int32, (tm, EXPERT_LANES), 1)
    lane4 = lax.broadcasted_iota(jnp.int32, (tm, TOP_K), 1)
    hits = [lane == idx[:, kk:kk + 1] for kk in range(TOP_K)]
    member = sum(jnp.where(hh, 1.0, 0.0) for hh in hits)
    before = _dot(tril_ref[...], member.astype(BF16)) + carry_ref[...]
    base = before + start_ref[...]
    dest = jnp.zeros((tm, TOP_K), jnp.int32)
    for kk in range(TOP_K):
        d = jnp.sum(jnp.where(hits[kk], base, 0.0), axis=-1, keepdims=True)
        dest = jnp.where(lane4 == kk, d.astype(jnp.int32), dest)
    dest_ref[...] = dest
    carry_ref[...] += jnp.sum(member, axis=0, keepdims=True)


def _route(idx, pad_start):
    T = idx.shape[0]
    tm = 512
    t_idx = jnp.arange(tm)
    tril = (t_idx[:, None] > t_idx[None, :]).astype(BF16)
    const = lambda i: (0, 0)
    return pl.pallas_call(
        _route_kernel,
        grid=(T // tm,),
        in_specs=[pl.BlockSpec((tm, TOP_K), lambda i: (i, 0)),
                  pl.BlockSpec((1, EXPERT_LANES), const),
                  pl.BlockSpec((tm, tm), const)],
        out_specs=pl.BlockSpec((tm, TOP_K), lambda i: (i, 0)),
        out_shape=jax.ShapeDtypeStruct((T, TOP_K), jnp.int32),
        scratch_shapes=[pltpu.VMEM((1, EXPERT_LANES), F32)],
        compiler_params=_cparams("arbitrary"),
        name="moe_route",
    )(idx, pad_start, tril)


DMA_ROWS_PER_STEP = 4
BLOCK_TILE_ROWS = MOE_BLOCK * ROW_SUBLANES


def _row_dma_loops(make_copy, n_rows):
    def issue(g, _):
        for u in range(DMA_ROWS_PER_STEP):
            for kk in range(TOP_K):
                make_copy(g * DMA_ROWS_PER_STEP + u, kk).start()
        return 0

    def drain(g, _):
        for u in range(DMA_ROWS_PER_STEP):
            for kk in range(TOP_K):
                make_copy(g * DMA_ROWS_PER_STEP + u, kk).wait()
        return 0

    lax.fori_loop(0, n_rows // DMA_ROWS_PER_STEP, issue, 0)
    lax.fori_loop(0, n_rows // DMA_ROWS_PER_STEP, drain, 0)


def _row_tile(ref, r):
    return ref.at[pl.ds(pl.multiple_of(r * ROW_SUBLANES, ROW_SUBLANES), ROW_SUBLANES), :]


def _scatter_kernel(zero_ref, dest_ref, x_ref, slots_ref, zbuf_ref, sem, zsem):
    @pl.when(pl.program_id(0) == 0)
    def _():
        zbuf_ref[...] = jnp.zeros_like(zbuf_ref)

        def zero_copy(z):
            first = pl.multiple_of(zero_ref[z] * BLOCK_TILE_ROWS, BLOCK_TILE_ROWS)
            return pltpu.make_async_copy(zbuf_ref, slots_ref.at[pl.ds(first, BLOCK_TILE_ROWS), :], zsem)

        for z in range(zero_ref.shape[0]):
            @pl.when(zero_ref[z] >= 0)
            def _():
                zero_copy(z).start()
        for z in range(zero_ref.shape[0]):
            @pl.when(zero_ref[z] >= 0)
            def _():
                zero_copy(z).wait()

    def row_copy(t, kk):
        return pltpu.make_async_copy(_row_tile(x_ref, t), _row_tile(slots_ref, dest_ref[t * TOP_K + kk]), sem)

    _row_dma_loops(row_copy, x_ref.shape[0] // ROW_SUBLANES)


def _scatter_rows(xn_tiles, dest_flat, zero_blocks, n_slots):
    T = xn_tiles.shape[0] // ROW_SUBLANES
    tm = 256
    grid_spec = pltpu.PrefetchScalarGridSpec(
        num_scalar_prefetch=1,
        grid=(T // tm,),
        in_specs=[pl.BlockSpec((tm * TOP_K,), lambda i, z: (i,), memory_space=pltpu.SMEM),
                  pl.BlockSpec((tm * ROW_SUBLANES, LANES), lambda i, z: (i, 0))],
        out_specs=pl.BlockSpec(memory_space=pl.ANY),
        scratch_shapes=[pltpu.VMEM((BLOCK_TILE_ROWS, LANES), F32),
                        pltpu.SemaphoreType.DMA(()), pltpu.SemaphoreType.DMA(())],
    )
    return pl.pallas_call(
        _scatter_kernel,
        grid_spec=grid_spec,
        out_shape=jax.ShapeDtypeStruct((n_slots * ROW_SUBLANES, LANES), F32),
        compiler_params=_cparams("arbitrary"),
        name="moe_scatter",
    )(zero_blocks, dest_flat, xn_tiles)


def _expert_kernel(be_ref, nu_ref, x_ref, wg_ref, bg_ref, wu_ref, bu_ref, wd_ref, bd_ref, y_ref):
    del be_ref
    used = pl.program_id(0) < nu_ref[0]

    @pl.when(jnp.logical_not(used))
    def _():
        y_ref[...] = jnp.zeros_like(y_ref)

    @pl.when(used)
    def _():
        xb = _load_row_tiles(x_ref, MOE_BLOCK).astype(BF16)
        gate = jnp.minimum(_dot(xb, wg_ref[0]) + bg_ref[0], SWIGLU_LIMIT)
        up = jnp.clip(_dot(xb, wu_ref[0]) + bu_ref[0], -SWIGLU_LIMIT, SWIGLU_LIMIT)
        glu = gate * jax.nn.sigmoid(SWIGLU_ALPHA * gate)
        _store_row_tiles(y_ref, _dot(((up + 1.0) * glu).astype(BF16), wd_ref[0]) + bd_ref[0])


def _expert_ffn(slots, block_expert, n_used, w_gate, b_gate, w_up, b_up, w_down, b_down):
    E, D, F = w_gate.shape
    n_blocks = slots.shape[0] // BLOCK_TILE_ROWS
    row = lambda i, be, nu: (jnp.minimum(i, nu[0] - 1), 0)
    wsel = lambda i, be, nu: (be[i], 0, 0)
    grid_spec = pltpu.PrefetchScalarGridSpec(
        num_scalar_prefetch=2,
        grid=(n_blocks,),
        in_specs=[pl.BlockSpec((BLOCK_TILE_ROWS, LANES), row),
                  pl.BlockSpec((1, D, F), wsel), pl.BlockSpec((1, 1, F), wsel),
                  pl.BlockSpec((1, D, F), wsel), pl.BlockSpec((1, 1, F), wsel),
                  pl.BlockSpec((1, F, D), wsel), pl.BlockSpec((1, 1, D), wsel)],
        out_specs=pl.BlockSpec((BLOCK_TILE_ROWS, LANES), lambda i, be, nu: (i, 0)),
    )
    return pl.pallas_call(
        _expert_kernel,
        grid_spec=grid_spec,
        out_shape=jax.ShapeDtypeStruct(slots.shape, F32),
        compiler_params=_cparams("arbitrary"),
        name="moe_experts",
    )(block_expert, n_used, slots, w_gate.astype(BF16), b_gate.reshape(E, 1, F), w_up.astype(BF16),
      b_up.reshape(E, 1, F), w_down.astype(BF16), b_down.reshape(E, 1, D))


def _combine_kernel(dest_ref, h_ref, gate_ref, g_ref, y_ref, o_ref, buf_ref, sem):
    tm = h_ref.shape[0]

    def row_copy(t, kk):
        return pltpu.make_async_copy(_row_tile(y_ref, dest_ref[t * TOP_K + kk]), _row_tile(buf_ref.at[kk], t), sem)

    _row_dma_loops(row_copy, tm)
    gates = gate_ref[...]
    h = h_ref[...]
    for kk in range(TOP_K):
        h = h + gates[:, kk:kk + 1] * _load_row_tiles(buf_ref.at[kk], tm)
    o_ref[...] = h * lax.rsqrt(jnp.mean(h * h, axis=-1, keepdims=True) + NORM_EPS) * g_ref[...]


def _combine(h1, gates, dest_flat, y_slots, norm_g):
    T, D = h1.shape
    tm = 256
    return pl.pallas_call(
        _combine_kernel,
        grid=(T // tm,),
        in_specs=[pl.BlockSpec((tm * TOP_K,), lambda i: (i,), memory_space=pltpu.SMEM),
                  pl.BlockSpec((tm, D), lambda i: (i, 0)),
                  pl.BlockSpec((tm, TOP_K), lambda i: (i, 0)),
                  pl.BlockSpec((1, D), lambda i: (0, 0)),
                  pl.BlockSpec(memory_space=pl.ANY)],
        out_specs=pl.BlockSpec((tm, D), lambda i: (i, 0)),
        out_shape=jax.ShapeDtypeStruct((T, D), F32),
        scratch_shapes=[pltpu.VMEM((TOP_K, tm * ROW_SUBLANES, LANES), F32), pltpu.SemaphoreType.DMA(())],
        compiler_params=_cparams("arbitrary"),
        name="moe_combine",
    )(dest_flat, h1, gates, norm_g.reshape(1, D), y_slots)


def _moe(h1, xn2, idx, gates, counts, w_gate, b_gate, w_up, b_up, w_down, b_down, norm_final_g):
    T, D = h1.shape
    E = w_gate.shape[0]
    n_blocks = T * TOP_K // MOE_BLOCK + E
    cnt = counts[0].astype(jnp.int32)
    padded = (cnt + MOE_BLOCK - 1) // MOE_BLOCK * MOE_BLOCK
    pad_end = jnp.cumsum(padded)
    pad_start = (pad_end - padded).astype(F32).reshape(1, EXPERT_LANES)
    block_first = jnp.arange(n_blocks, dtype=jnp.int32) * MOE_BLOCK
    block_expert = jnp.minimum(
        jnp.sum((pad_end[None, :E] <= block_first[:, None]).astype(jnp.int32), axis=1), E - 1)
    n_used = (pad_end[E - 1] // MOE_BLOCK).astype(jnp.int32).reshape(1)
    last_block = jnp.where(padded[:E] > 0, pad_end[:E] // MOE_BLOCK - 1, -1)
    spare = n_used[0] + jnp.arange(E, dtype=jnp.int32)
    zero_blocks = jnp.concatenate([last_block, jnp.where(spare < n_blocks, spare, -1)]).astype(jnp.int32)
    dest = _route(idx, pad_start).reshape(T * TOP_K)
    slots = _scatter_rows(xn2, dest, zero_blocks, n_blocks * MOE_BLOCK)
    y_slots = _expert_ffn(slots, block_expert, n_used, w_gate, b_gate, w_up, b_up, w_down, b_down)
    return _combine(h1, gates, dest, y_slots, norm_final_g)


def kernel(x, norm_mix_g, w_in, attn_norm_g, rwkv_mu, rwkv_w0, rwkv_w2, rwkv_a0, rwkv_a2,
           rwkv_g2, rwkv_k_k, rwkv_k_a, rwkv_r_k, rwkv_ln_g, rwkv_ln_b, w_out, norm_ffn_g,
           w_router, b_router, moe_w_gate, moe_b_gate, moe_w_up, moe_b_up, moe_w_down,
           moe_b_down, norm_final_g):
    B, S, D = x.shape
    x2 = x.reshape(B * S, D)
    q, k, vt, km, xs = _in_projection(x2, norm_mix_g[0], w_in[0], rwkv_mu[0], B, S)
    o_att = _moba_attention(q, k, vt, km, attn_norm_g[0], B, S)
    o_rwkv = _rwkv_mix(xs, rwkv_w0[0], rwkv_w2[0], rwkv_a0[0], rwkv_a2[0], rwkv_g2[0], rwkv_k_k[0],
                       rwkv_k_a[0], rwkv_r_k[0], rwkv_ln_g[0], rwkv_ln_b[0], B, S)
    h1, xn2, idx, gates, counts = _outproj_router(o_att, o_rwkv, x2, w_out[0], norm_ffn_g[0],
                                                  w_router[0], b_router[0])
    out = _moe(h1, xn2, idx, gates, counts, moe_w_gate[0], moe_b_gate[0], moe_w_up[0], moe_b_up[0],
               moe_w_down[0], moe_b_down[0], norm_final_g)
    return out.reshape(B, S, D)
```

```python
import functools
import math

import jax
import jax.numpy as jnp
from jax import lax
from jax.experimental import pallas as pl
from jax.experimental.pallas import tpu as pltpu

F32 = jnp.float32
BF16 = jnp.bfloat16

HEAD_DIM = 64
N_HEADS = 8
GROUP_WIDTH = N_HEADS * HEAD_DIM
ATTN_TILE_WIDTH = N_HEADS * 2 * HEAD_DIM
MOBA_BLOCK = 256
MOBA_TOPK = 3
N_EXPERTS = 32
TOP_K = 4
MOE_BLOCK = 256
SWIGLU_LIMIT = 7.0
SWIGLU_ALPHA = 1.702
NORM_EPS = 1e-6
NEG_INF = -1e30
RWKV_DECAY_SCALE = math.exp(-0.5)
RWKV_GN_EPS = HEAD_DIM * 1e-5
RWKV_LORA = (32, 32, 96)
RWKV_LORA_PAD = 256
RWKV_CHUNK = 64
RWKV_CHUNKS_PER_STEP = 4

VMEM_LIMIT = 48 * 1024 * 1024


def _cparams(*sem):
    return pltpu.CompilerParams(dimension_semantics=sem, vmem_limit_bytes=VMEM_LIMIT)


def _dot(a, b):
    return jnp.dot(a, b, preferred_element_type=F32)


def _dot_nt(a, b):
    return lax.dot_general(a, b, (((1,), (1,)), ((), ())), preferred_element_type=F32)


def _dot_tn(a, b):
    return lax.dot_general(a, b, (((0,), (0,)), ((), ())), preferred_element_type=F32)


def _split2(x):
    hi = x.astype(BF16)
    lo = (x - hi.astype(F32)).astype(BF16)
    return hi, lo


def _inproj_kernel(x_ref, g_ref, wq_ref, wk_ref, wvt_ref, wr_ref, mu_ref, qaug_ref, kaug_ref,
                   q_ref, k_ref, vt_ref, kmean_ref, xs_ref, carry_ref, *, tiles_per_seq):
    i = pl.program_id(0)
    x = x_ref[...]
    xn = x * lax.rsqrt(jnp.mean(x * x, axis=-1, keepdims=True) + NORM_EPS) * g_ref[...]
    xb = xn.astype(BF16)
    q_ref[...] = (_dot(xb, wq_ref[...]) * (HEAD_DIM ** -0.5) + qaug_ref[...]).astype(BF16)
    k = _dot(xb, wk_ref[...])
    k_ref[...] = (k + kaug_ref[...]).astype(BF16)
    kmean_ref[0] = jnp.mean(k, axis=0, keepdims=True)
    vt_ref[0] = _dot_nt(wvt_ref[...], xb).astype(BF16)
    pr = _dot(xb, wr_ref[...])

    @pl.when(i % tiles_per_seq == 0)
    def _():
        carry_ref[...] = jnp.zeros_like(carry_ref)

    tm = pr.shape[0]
    rolled = pltpu.roll(pr, 1, 0)
    row = lax.broadcasted_iota(jnp.int32, pr.shape, 0)
    prev = jnp.where(row == 0, carry_ref[...], rolled)
    carry_ref[...] = pr[tm - 1:tm, :]
    xs_ref[...] = pr + (prev - pr) * mu_ref[...]


def _in_projection(x2, norm_g, w_in, mu, batch, seq):
    T, D = x2.shape
    tm = MOBA_BLOCK
    nb = seq // tm
    GW = GROUP_WIDTH
    AW = ATTN_TILE_WIDTH

    def spread(w):
        w = w.reshape(D, N_HEADS, HEAD_DIM)
        return jnp.pad(w, ((0, 0), (0, 0), (0, HEAD_DIM))).reshape(D, AW).astype(BF16)

    wq = spread(w_in[:, :GW])
    wk = spread(w_in[:, GW:2 * GW])
    pos = jnp.arange(tm, dtype=F32)[:, None, None]
    slope = jnp.asarray(_alibi_slopes(), F32)[None, :, None]
    lane = jnp.arange(2 * HEAD_DIM)[None, None, :]
    qaug = jnp.where(lane == HEAD_DIM, -slope * pos, jnp.where(lane == HEAD_DIM + 1, 1.0, 0.0)).reshape(tm, AW)
    kaug = jnp.where(lane == HEAD_DIM, 1.0, jnp.where(lane == HEAD_DIM + 1, slope * pos, 0.0)).reshape(tm, AW)
    wvt = w_in[:, 2 * GW:3 * GW].T.astype(BF16)
    rw = 3 * GW + RWKV_LORA_PAD
    n_r = w_in.shape[1] - 3 * GW
    wr = jnp.pad(w_in[:, 3 * GW:], ((0, 0), (0, rw - n_r))).astype(BF16)
    mu_p = jnp.pad(mu, (0, rw - n_r)).reshape(1, rw)
    const = lambda i: (0, 0)
    return pl.pallas_call(
        functools.partial(_inproj_kernel, tiles_per_seq=nb),
        grid=(T // tm,),
        in_specs=[
            pl.BlockSpec((tm, D), lambda i: (i, 0)),
            pl.BlockSpec((1, D), const),
            pl.BlockSpec((D, AW), const),
            pl.BlockSpec((D, AW), const),
            pl.BlockSpec((GW, D), const),
            pl.BlockSpec((D, rw), const),
            pl.BlockSpec((1, rw), const),
            pl.BlockSpec((tm, AW), const),
            pl.BlockSpec((tm, AW), const),
        ],
        out_specs=[
            pl.BlockSpec((tm, AW), lambda i: (i, 0)),
            pl.BlockSpec((tm, AW), lambda i: (i, 0)),
            pl.BlockSpec((1, GW, tm), lambda i: (i, 0, 0)),
            pl.BlockSpec((1, 1, AW), lambda i: (i, 0, 0)),
            pl.BlockSpec((tm, rw), lambda i: (i, 0)),
        ],
        out_shape=[
            jax.ShapeDtypeStruct((T, AW), BF16),
            jax.ShapeDtypeStruct((T, AW), BF16),
            jax.ShapeDtypeStruct((T // tm, GW, tm), BF16),
            jax.ShapeDtypeStruct((T // tm, 1, AW), F32),
            jax.ShapeDtypeStruct((T, rw), F32),
        ],
        scratch_shapes=[pltpu.VMEM((1, rw), F32)],
        compiler_params=_cparams("arbitrary"),
        name="in_projection",
    )(x2, norm_g.reshape(1, D), wq, wk, wvt, wr, mu_p, qaug, kaug)


def _moba_kernel(q_ref, k_ref, vt_ref, kmean_ref, g_ref,
                 o_ref, selb_ref, m_ref, l_ref, acc_ref, *, nb, nbp, slopes):
    i = pl.program_id(1)
    blk = MOBA_BLOCK
    heads = range(N_HEADS)
    tile = [slice(h * 2 * HEAD_DIM, (h + 1) * 2 * HEAD_DIM) for h in heads]
    vrow = [slice(h * HEAD_DIM, (h + 1) * HEAD_DIM) for h in heads]
    n_iota = lax.broadcasted_iota(jnp.int32, (nbp, blk), 0)
    valid = n_iota < i

    gates = []
    for h in heads:
        km_hi, km_lo = _split2(kmean_ref[0, :, tile[h]])
        qh = q_ref[:, tile[h]]
        gates.append(jnp.where(valid, _dot_nt(km_hi, qh) + _dot_nt(km_lo, qh), NEG_INF))
    for h in heads:
        rank = jnp.zeros((nbp, blk), jnp.int32)
        for m in range(nb):
            gm = gates[h][m:m + 1, :]
            tie = (n_iota > m).astype(jnp.int32)
            rank = rank + jnp.where(gm > gates[h], 1, jnp.where(gm == gates[h], tie, 0))
        selb_ref[h] = jnp.where(rank < MOBA_TOPK, jnp.where(valid, 0.0, NEG_INF), NEG_INF)

    own = pl.ds(pl.multiple_of(i * blk, blk), blk)
    future = (lax.broadcasted_iota(jnp.int32, (blk, blk), 0) >
              lax.broadcasted_iota(jnp.int32, (blk, blk), 1))
    st = [jnp.where(future, NEG_INF, _dot_nt(k_ref[0, own, tile[h]], q_ref[:, tile[h]])) for h in heads]
    ps = []
    for h in heads:
        m0 = jnp.max(st[h], axis=0, keepdims=True)
        p = jnp.exp(st[h] - m0)
        m_ref[h] = m0
        l_ref[h] = jnp.sum(p, axis=0, keepdims=True)
        ps.append(p.astype(BF16))
    for h in heads:
        acc_ref[vrow[h], :] = _dot(vt_ref[0, i, vrow[h], :], ps[h])

    def body(j, _):
        rows = pl.ds(pl.multiple_of(j * blk, blk), blk)
        dist = (i - j).astype(F32) * blk
        st = [_dot_nt(k_ref[0, rows, tile[h]], q_ref[:, tile[h]]) for h in heads]
        ps, alphas = [], []
        for h in heads:
            off = selb_ref[h, pl.ds(j, 1), :] - slopes[h] * dist
            m_old = m_ref[h]
            m_new = jnp.maximum(m_old, jnp.max(st[h], axis=0, keepdims=True) + off)
            alpha = jnp.exp(m_old - m_new)
            p = jnp.exp(st[h] - (m_new - off))
            m_ref[h] = m_new
            l_ref[h] = alpha * l_ref[h] + jnp.sum(p, axis=0, keepdims=True)
            ps.append(p.astype(BF16))
            alphas.append(alpha)
        for h in heads:
            acc_ref[vrow[h], :] = alphas[h] * acc_ref[vrow[h], :] + _dot(vt_ref[0, j, vrow[h], :], ps[h])
        return 0

    lax.fori_loop(0, i, body, 0)

    for h in heads:
        o = acc_ref[vrow[h], :] / l_ref[h]
        acc_ref[vrow[h], :] = o * lax.rsqrt(jnp.mean(o * o, axis=0, keepdims=True) + NORM_EPS)
    o_ref[...] = (acc_ref[...].T * g_ref[...]).astype(o_ref.dtype)


def _alibi_slopes():
    return [2.0 ** (-8.0 * (h + 1) / N_HEADS) for h in range(N_HEADS)]


def _moba_attention(q, k, vt, kmean, attn_norm_g, batch, seq):
    GW = GROUP_WIDTH
    AW = ATTN_TILE_WIDTH
    blk = MOBA_BLOCK
    nb = seq // blk
    nbp = -(-nb // 16) * 16
    k3 = k.reshape(batch, seq, AW)
    vt4 = vt.reshape(batch, nb, GW, blk)
    km = jnp.pad(kmean.reshape(batch, nb, AW), ((0, 0), (0, nbp - nb), (0, 0)))
    return pl.pallas_call(
        functools.partial(_moba_kernel, nb=nb, nbp=nbp, slopes=_alibi_slopes()),
        grid=(batch, nb),
        in_specs=[
            pl.BlockSpec((blk, AW), lambda b, i: (b * nb + i, 0)),
            pl.BlockSpec((1, seq, AW), lambda b, i: (b, 0, 0)),
            pl.BlockSpec((1, nb, GW, blk), lambda b, i: (b, 0, 0, 0)),
            pl.BlockSpec((1, nbp, AW), lambda b, i: (b, 0, 0)),
            pl.BlockSpec((1, GW), lambda b, i: (0, 0)),
        ],
        out_specs=pl.BlockSpec((blk, GW), lambda b, i: (b * nb + i, 0)),
        out_shape=jax.ShapeDtypeStruct((batch * seq, GW), BF16),
        scratch_shapes=[pltpu.VMEM((N_HEADS, nbp, blk), F32),
                        pltpu.VMEM((N_HEADS, 1, blk), F32),
                        pltpu.VMEM((N_HEADS, 1, blk), F32),
                        pltpu.VMEM((GW, blk), F32)],
        compiler_params=_cparams("arbitrary", "arbitrary"),
        name="moba_attention",
    )(q, k3, vt4, km, attn_norm_g.reshape(1, GW))


def _mm(a, b, dims=((1,), (0,)), passes=3):
    dn = (dims, ((), ()))
    dg = lambda u, v: lax.dot_general(u, v, dn, preferred_element_type=F32)
    if passes == 1:
        return dg(a.astype(BF16), b.astype(BF16))
    a_hi, a_lo = _split2(a)
    b_hi, b_lo = _split2(b)
    return dg(a_hi, b_hi) + dg(a_hi, b_lo) + dg(a_lo, b_hi)


_NT = ((1,), (1,))
_TN = ((0,), (0,))
RWKV_PASSES = {"scores": 1, "inverse": 1, "outer": 1, "state": 1}


def _head_sum(x, pair_ones):
    x_hi, x_lo = _split2(x)
    w = pair_ones.shape[0]
    tiles = [_dot(x_hi[:, t:t + w], pair_ones) + _dot(x_lo[:, t:t + w], pair_ones)
             for t in range(0, x.shape[1], w)]
    return jnp.concatenate(tiles, axis=1)


def _rwkv_kernel(xs_ref, w2_ref, a2_ref, g2_ref, vec_ref, tril_ref, hsum_ref,
                 o_ref, state_ref, cum_ref, lw_ref, r_ref, kk_ref, b_ref, k2_ref, v_ref, oc_ref):
    GW = GROUP_WIDTH
    L = RWKV_CHUNK
    N = HEAD_DIM
    tm = xs_ref.shape[0]
    w0, a0, k_k, k_a, r_k, ln_g, ln_b = [vec_ref[n:n + 1, :] for n in range(7)]

    @pl.when(pl.program_id(1) == 0)
    def _():
        state_ref[...] = jnp.zeros_like(state_ref)

    r = xs_ref[:, 0:GW]
    k = xs_ref[:, GW:2 * GW]
    v = xs_ref[:, 2 * GW:3 * GW]
    lo = xs_ref[:, 3 * GW:]
    hsum = hsum_ref[...]
    logw = -RWKV_DECAY_SCALE * jax.nn.sigmoid(w0 + _dot(jnp.tanh(lo).astype(BF16), w2_ref[...]))
    a = jax.nn.sigmoid(a0 + _dot(lo.astype(BF16), a2_ref[...]))
    gate = _dot(jax.nn.sigmoid(lo).astype(BF16), g2_ref[...])
    kk = k * k_k
    kk = kk * lax.rsqrt(jnp.maximum(_head_sum(kk * kk, hsum), 1e-24))
    k2 = k * (1.0 + (a - 1.0) * k_a)
    bonus = _head_sum(r * k2 * r_k, hsum) * v
    lw_hi, lw_lo = _split2(logw)
    cum_ref[...] = _dot(tril_ref[...], lw_hi) + _dot(tril_ref[...], lw_lo)
    lw_ref[...] = logw
    r_ref[...] = r
    kk_ref[...] = kk
    b_ref[...] = kk * a
    k2_ref[...] = k2
    v_ref[...] = v

    row = lax.broadcasted_iota(jnp.int32, (L, L), 0)
    col = lax.broadcasted_iota(jnp.int32, (L, L), 1)
    strict = row > col
    incl = row >= col
    eye = row == col

    U = RWKV_CHUNKS_PER_STEP
    hsl = [slice(h * N, (h + 1) * N) for h in range(N_HEADS)]
    units = [(u, h) for u in range(U) for h in range(N_HEADS)]

    def step(c, _):
        pre = []
        for u in range(U):
            rows = pl.ds(pl.multiple_of((c * U + u) * L, L), L)
            cum = cum_ref[rows, :]
            cum_end = cum[L - 1:L, :]
            w_incl = jnp.exp(cum)
            w_excl = jnp.exp(cum - lw_ref[rows, :])
            w_inv = jnp.exp(-cum)
            e_end = jnp.exp(cum_end - cum)
            pre.append(dict(
                rows=rows, w_end=jnp.exp(cum_end),
                rhat=r_ref[rows, :] * w_incl, kkhat=kk_ref[rows, :] * w_excl,
                btil=b_ref[rows, :] * w_inv, ktil=k2_ref[rows, :] * w_inv,
                bbar=b_ref[rows, :] * e_end, kbar=k2_ref[rows, :] * e_end, v=v_ref[rows, :]))
        sl = lambda name, u, h: pre[u][name][:, hsl[h]]
        ma = [_mm(jnp.concatenate([sl("kkhat", u, h), sl("rhat", u, h)], axis=0),
                  jnp.concatenate([sl("btil", u, h), sl("ktil", u, h)], axis=0), _NT, RWKV_PASSES["scores"])
              for u, h in units]
        m_bk = [jnp.where(strict, m[:L, :L], 0.0) for m in ma]
        m_kk = [jnp.where(strict, m[:L, L:], 0.0) for m in ma]
        a_br = [jnp.where(incl, m[L:, :L], 0.0) for m in ma]
        a_kr = [jnp.where(incl, m[L:, L:], 0.0) for m in ma]
        n_u = range(len(units))
        vh = [sl("v", u, h) for u, h in units]
        pi = RWKV_PASSES["inverse"]
        y = [jnp.concatenate([sl("kkhat", *units[n]), _mm(m_kk[n], vh[n], passes=pi)], axis=1) for n in n_u]
        y = [y[n] - _mm(m_bk[n], y[n], passes=pi) for n in n_u]
        pw = m_bk
        for _ in range(int(math.log2(L)) - 1):
            pw = [_mm(pw[n], pw[n], passes=pi) for n in n_u]
            y = [y[n] + _mm(pw[n], y[n], passes=pi) for n in n_u]
        po = RWKV_PASSES["outer"]
        xtb = [_mm(y[n], sl("bbar", *units[n]), _TN, po) for n in n_u]
        vtk = [_mm(vh[n], sl("kbar", *units[n]), _TN, po) for n in n_u]
        ax = [_mm(a_br[n], y[n], passes=po) for n in n_u]
        akv = [_mm(a_kr[n], vh[n], passes=po) for n in n_u]
        ps = RWKV_PASSES["state"]
        state = [state_ref[h] for h in range(N_HEADS)]
        for n, (u, h) in enumerate(units):
            r2 = sl("rhat", u, h) - ax[n][:, :N]
            oc_ref[pre[u]["rows"], hsl[h]] = _mm(r2, state[h], _NT, ps) + akv[n] - ax[n][:, N:]
            p_mat = jnp.where(eye, pre[u]["w_end"][:, hsl[h]], 0.0) - xtb[n][:N]
            state[h] = _mm(state[h], p_mat, passes=ps) + vtk[n] - xtb[n][N:]
        for h in range(N_HEADS):
            state_ref[h] = state[h]
        return 0

    lax.fori_loop(0, tm // (L * U), step, 0)

    o = oc_ref[...]
    mean = _head_sum(o, hsum) * (1.0 / N)
    d = o - mean
    var = _head_sum(d * d, hsum) * (1.0 / N)
    o = d * lax.rsqrt(var + RWKV_GN_EPS) * ln_g + ln_b
    o_ref[...] = ((o + bonus) * gate).astype(o_ref.dtype)


def _rwkv_mix(xs, w0, w2, a0, a2, g2, k_k, k_a, r_k, ln_g, ln_b, batch, seq):
    T, rw = xs.shape
    GW = GROUP_WIDTH
    L = RWKV_CHUNK
    tm = 256
    nl = rw - 3 * GW
    d0, d1, d2 = RWKV_LORA
    w2p = jnp.pad(w2, ((0, nl - d0), (0, 0))).astype(BF16)
    a2p = jnp.pad(a2, ((d0, nl - d0 - d1), (0, 0))).astype(BF16)
    g2p = jnp.pad(g2, ((d0 + d1, nl - d0 - d1 - d2), (0, 0))).astype(BF16)
    vecs = jnp.stack([w0, a0, k_k, k_a, r_k.reshape(GW), ln_g, ln_b, jnp.zeros((GW,), F32)])
    t_idx = jnp.arange(tm)
    tril = ((t_idx[:, None] >= t_idx[None, :]) & (t_idx[:, None] // L == t_idx[None, :] // L)).astype(BF16)
    c_idx = jnp.arange(2 * HEAD_DIM) // HEAD_DIM
    hsum = (c_idx[:, None] == c_idx[None, :]).astype(BF16)
    tps = seq // tm
    const = lambda b, i: (0, 0)
    scr = lambda: pltpu.VMEM((tm, GW), F32)
    return pl.pallas_call(
        _rwkv_kernel,
        grid=(batch, tps),
        in_specs=[
            pl.BlockSpec((tm, rw), lambda b, i: (b * tps + i, 0)),
            pl.BlockSpec((nl, GW), const),
            pl.BlockSpec((nl, GW), const),
            pl.BlockSpec((nl, GW), const),
            pl.BlockSpec((8, GW), const),
            pl.BlockSpec((tm, tm), const),
            pl.BlockSpec((2 * HEAD_DIM, 2 * HEAD_DIM), const),
        ],
        out_specs=pl.BlockSpec((tm, GW), lambda b, i: (b * tps + i, 0)),
        out_shape=jax.ShapeDtypeStruct((T, GW), BF16),
        scratch_shapes=[pltpu.VMEM((N_HEADS, HEAD_DIM, HEAD_DIM), F32)] + [scr() for _ in range(8)],
        compiler_params=_cparams("arbitrary", "arbitrary"),
        name="rwkv7_mix",
    )(xs, w2p, a2p, g2p, vecs, tril, hsum)


EXPERT_LANES = 128

ROW_SUBLANES = 8
LANES = 128


def _store_row_tiles(ref, x):
    n = x.shape[0]
    for s in range(ROW_SUBLANES):
        ref[pl.ds(s, n, stride=ROW_SUBLANES), :] = x[:, s * LANES:(s + 1) * LANES]


def _load_row_tiles(ref, n):
    return jnp.concatenate([ref[pl.ds(s, n, stride=ROW_SUBLANES), :] for s in range(ROW_SUBLANES)], axis=1)


def _outproj_router_kernel(oa_ref, or_ref, x_ref, wa_ref, wr_ref, g_ref, wrt_ref, brt_ref,
                           h_ref, xn_ref, idx_ref, gate_ref, cnt_ref):
    h = x_ref[...] + _dot(oa_ref[...], wa_ref[...]) + _dot(or_ref[...], wr_ref[...])
    h_ref[...] = h
    xn = h * lax.rsqrt(jnp.mean(h * h, axis=-1, keepdims=True) + NORM_EPS) * g_ref[...]
    _store_row_tiles(xn_ref, xn)
    logits = _mm(xn, wrt_ref[...]) + brt_ref[...]
    tm = logits.shape[0]
    lane = lax.broadcasted_iota(jnp.int32, logits.shape, 1)
    lane4 = lax.broadcasted_iota(jnp.int32, (tm, TOP_K), 1)
    idx_out = jnp.zeros((tm, TOP_K), jnp.int32)
    val_out = jnp.zeros((tm, TOP_K), F32)
    member = jnp.zeros(logits.shape, F32)
    top = None
    denom = jnp.zeros((tm, 1), F32)
    for kk in range(TOP_K):
        mx = jnp.max(logits, axis=-1, keepdims=True)
        idx = jnp.min(jnp.where(logits == mx, lane, EXPERT_LANES), axis=-1, keepdims=True)
        hit = lane == idx
        member = jnp.where(hit, 1.0, member)
        logits = jnp.where(hit, -jnp.inf, logits)
        top = mx if top is None else top
        e = jnp.exp(mx - top)
        denom = denom + e
        idx_out = jnp.where(lane4 == kk, idx, idx_out)
        val_out = jnp.where(lane4 == kk, e, val_out)
    idx_ref[...] = idx_out
    gate_ref[...] = val_out / denom

    @pl.when(pl.program_id(0) == 0)
    def _():
        cnt_ref[...] = jnp.zeros_like(cnt_ref)

    cnt_ref[...] += jnp.sum(member, axis=0, keepdims=True)


def _outproj_router(o_att, o_rwkv, x2, w_out, norm_g, w_router, b_router):
    T, D = x2.shape
    GW = GROUP_WIDTH
    tm = 512
    E = w_router.shape[1]
    wa = w_out[:GW].astype(BF16)
    wr = w_out[GW:].astype(BF16)
    wrt = jnp.pad(w_router, ((0, 0), (0, EXPERT_LANES - E)))
    brt = jnp.pad(b_router, (0, EXPERT_LANES - E), constant_values=NEG_INF).reshape(1, EXPERT_LANES)
    const = lambda i: (0, 0)
    tile = lambda w: pl.BlockSpec((tm, w), lambda i: (i, 0))
    return pl.pallas_call(
        _outproj_router_kernel,
        grid=(T // tm,),
        in_specs=[tile(GW), tile(GW), tile(D),
                  pl.BlockSpec((GW, D), const), pl.BlockSpec((GW, D), const),
                  pl.BlockSpec((1, D), const), pl.BlockSpec((D, EXPERT_LANES), const),
                  pl.BlockSpec((1, EXPERT_LANES), const)],
        out_specs=[tile(D), pl.BlockSpec((tm * ROW_SUBLANES, LANES), lambda i: (i, 0)),
                   tile(TOP_K), tile(TOP_K), pl.BlockSpec((1, EXPERT_LANES), const)],
        out_shape=[jax.ShapeDtypeStruct((T, D), F32), jax.ShapeDtypeStruct((T * ROW_SUBLANES, LANES), F32),
                   jax.ShapeDtypeStruct((T, TOP_K), jnp.int32), jax.ShapeDtypeStruct((T, TOP_K), F32),
                   jax.ShapeDtypeStruct((1, EXPERT_LANES), F32)],
        compiler_params=_cparams("arbitrary"),
        name="outproj_router",
    )(o_att, o_rwkv, x2, wa, wr, norm_g.reshape(1, D), wrt, brt)


def _route_kernel(idx_ref, start_ref, tril_ref, dest_ref, carry_ref):
    @pl.when(pl.program_id(0) == 0)
    def _():
        carry_ref[...] = jnp.zeros_like(carry_ref)

    idx = idx_ref[...]
    tm = idx.shape[0]
    lane = lax.broadcasted_iota(jnp.int32, (tm, EXPERT_LANES), 1)
    lane4 = lax.broadcasted_iota(jnp.int32, (tm, TOP_K), 1)
    hits = [lane == idx[:, kk:kk + 1] for kk in range(TOP_K)]
    member = sum(jnp.where(hh, 1.0, 0.0) for hh in hits)
    before = _dot(tril_ref[...], member.astype(BF16)) + carry_ref[...]
    base = before + start_ref[...]
    dest = jnp.zeros((tm, TOP_K), jnp.int32)
    for kk in range(TOP_K):
        d = jnp.sum(jnp.where(hits[kk], base, 0.0), axis=-1, keepdims=True)
        dest = jnp.where(lane4 == kk, d.astype(jnp.int32), dest)
    dest_ref[...] = dest
    carry_ref[...] += jnp.sum(member, axis=0, keepdims=True)


def _route(idx, pad_start):
    T = idx.shape[0]
    tm = 512
    t_idx = jnp.arange(tm)
    tril = (t_idx[:, None] > t_idx[None, :]).astype(BF16)
    const = lambda i: (0, 0)
    return pl.pallas_call(
        _route_kernel,
        grid=(T // tm,),
        in_specs=[pl.BlockSpec((tm, TOP_K), lambda i: (i, 0)),
                  pl.BlockSpec((1, EXPERT_LANES), const),
                  pl.BlockSpec((tm, tm), const)],
        out_specs=pl.BlockSpec((tm, TOP_K), lambda i: (i, 0)),
        out_shape=jax.ShapeDtypeStruct((T, TOP_K), jnp.int32),
        scratch_shapes=[pltpu.VMEM((1, EXPERT_LANES), F32)],
        compiler_params=_cparams("arbitrary"),
        name="moe_route",
    )(idx, pad_start, tril)


DMA_ROWS_PER_STEP = 4
BLOCK_TILE_ROWS = MOE_BLOCK * ROW_SUBLANES


def _start_row_copies(make_copy, n_rows):
    def issue(g, _):
        for u in range(DMA_ROWS_PER_STEP):
            for kk in range(TOP_K):
                make_copy(g * DMA_ROWS_PER_STEP + u, kk).start(priority=kk % 2)
        return 0

    lax.fori_loop(0, n_rows // DMA_ROWS_PER_STEP, issue, 0)


def _wait_row_copies(make_copy, n_rows):
    def drain(g, _):
        for u in range(DMA_ROWS_PER_STEP):
            for kk in range(TOP_K):
                make_copy(g * DMA_ROWS_PER_STEP + u, kk).wait()
        return 0

    lax.fori_loop(0, n_rows // DMA_ROWS_PER_STEP, drain, 0)


def _row_tile(ref, r):
    return ref.at[pl.ds(pl.multiple_of(r * ROW_SUBLANES, ROW_SUBLANES), ROW_SUBLANES), :]


def _scatter_kernel(zero_ref, dest_ref, x_ref, slots_ref, zbuf_ref, sem, zsem):
    @pl.when(pl.program_id(0) == 0)
    def _():
        zbuf_ref[...] = jnp.zeros_like(zbuf_ref)

        def zero_copy(z):
            first = pl.multiple_of(zero_ref[z] * BLOCK_TILE_ROWS, BLOCK_TILE_ROWS)
            return pltpu.make_async_copy(zbuf_ref, slots_ref.at[pl.ds(first, BLOCK_TILE_ROWS), :], zsem)

        for z in range(zero_ref.shape[0]):
            @pl.when(zero_ref[z] >= 0)
            def _():
                zero_copy(z).start()
        for z in range(zero_ref.shape[0]):
            @pl.when(zero_ref[z] >= 0)
            def _():
                zero_copy(z).wait()

    def row_copy(t, kk):
        return pltpu.make_async_copy(_row_tile(x_ref, t), _row_tile(slots_ref, dest_ref[t * TOP_K + kk]), sem)

    n_rows = x_ref.shape[0] // ROW_SUBLANES
    _start_row_copies(row_copy, n_rows)
    _wait_row_copies(row_copy, n_rows)


def _scatter_rows(xn_tiles, dest_flat, zero_blocks, n_slots):
    T = xn_tiles.shape[0] // ROW_SUBLANES
    tm = 256
    grid_spec = pltpu.PrefetchScalarGridSpec(
        num_scalar_prefetch=1,
        grid=(T // tm,),
        in_specs=[pl.BlockSpec((tm * TOP_K,), lambda i, z: (i,), memory_space=pltpu.SMEM),
                  pl.BlockSpec((tm * ROW_SUBLANES, LANES), lambda i, z: (i, 0))],
        out_specs=pl.BlockSpec(memory_space=pl.ANY),
        scratch_shapes=[pltpu.VMEM((BLOCK_TILE_ROWS, LANES), F32),
                        pltpu.SemaphoreType.DMA(()), pltpu.SemaphoreType.DMA(())],
    )
    return pl.pallas_call(
        _scatter_kernel,
        grid_spec=grid_spec,
        out_shape=jax.ShapeDtypeStruct((n_slots * ROW_SUBLANES, LANES), F32),
        compiler_params=_cparams("arbitrary"),
        name="moe_scatter",
    )(zero_blocks, dest_flat, xn_tiles)


def _expert_kernel(be_ref, nu_ref, x_ref, wg_ref, bg_ref, wu_ref, bu_ref, wd_ref, bd_ref, y_ref):
    del be_ref
    used = pl.program_id(0) < nu_ref[0]

    @pl.when(jnp.logical_not(used))
    def _():
        y_ref[...] = jnp.zeros_like(y_ref)

    @pl.when(used)
    def _():
        xb = _load_row_tiles(x_ref, MOE_BLOCK).astype(BF16)
        gate = jnp.minimum(_dot(xb, wg_ref[0]) + bg_ref[0], SWIGLU_LIMIT)
        up = jnp.clip(_dot(xb, wu_ref[0]) + bu_ref[0], -SWIGLU_LIMIT, SWIGLU_LIMIT)
        glu = gate * jax.nn.sigmoid(SWIGLU_ALPHA * gate)
        _store_row_tiles(y_ref, _dot(((up + 1.0) * glu).astype(BF16), wd_ref[0]) + bd_ref[0])


def _expert_ffn(slots, block_expert, n_used, w_gate, b_gate, w_up, b_up, w_down, b_down):
    E, D, F = w_gate.shape
    n_blocks = slots.shape[0] // BLOCK_TILE_ROWS
    row = lambda i, be, nu: (jnp.minimum(i, nu[0] - 1), 0)
    wsel = lambda i, be, nu: (be[i], 0, 0)
    grid_spec = pltpu.PrefetchScalarGridSpec(
        num_scalar_prefetch=2,
        grid=(n_blocks,),
        in_specs=[pl.BlockSpec((BLOCK_TILE_ROWS, LANES), row),
                  pl.BlockSpec((1, D, F), wsel), pl.BlockSpec((1, 1, F), wsel),
                  pl.BlockSpec((1, D, F), wsel), pl.BlockSpec((1, 1, F), wsel),
                  pl.BlockSpec((1, F, D), wsel), pl.BlockSpec((1, 1, D), wsel)],
        out_specs=pl.BlockSpec((BLOCK_TILE_ROWS, LANES), lambda i, be, nu: (i, 0)),
    )
    return pl.pallas_call(
        _expert_kernel,
        grid_spec=grid_spec,
        out_shape=jax.ShapeDtypeStruct(slots.shape, F32),
        compiler_params=_cparams("arbitrary"),
        name="moe_experts",
    )(block_expert, n_used, slots, w_gate.astype(BF16), b_gate.reshape(E, 1, F), w_up.astype(BF16),
      b_up.reshape(E, 1, F), w_down.astype(BF16), b_down.reshape(E, 1, D))


def _combine_kernel(dest_ref, dest_next_ref, h_ref, gate_ref, g_ref, y_ref, o_ref, buf_ref, sem):
    i = pl.program_id(0)
    tm = h_ref.shape[0]
    slot = i % 2

    def gather(dref, s):
        def row_copy(t, kk):
            return pltpu.make_async_copy(_row_tile(y_ref, dref[t * TOP_K + kk]),
                                         _row_tile(buf_ref.at[s, kk], t), sem.at[s])
        return row_copy

    @pl.when(i == 0)
    def _():
        _start_row_copies(gather(dest_ref, 0), tm)

    @pl.when(i + 1 < pl.num_programs(0))
    def _():
        _start_row_copies(gather(dest_next_ref, 1 - slot), tm)

    _wait_row_copies(gather(dest_ref, slot), tm)
    gates = gate_ref[...]
    h = h_ref[...]
    for kk in range(TOP_K):
        h = h + gates[:, kk:kk + 1] * _load_row_tiles(buf_ref.at[slot, kk], tm)
    o_ref[...] = h * lax.rsqrt(jnp.mean(h * h, axis=-1, keepdims=True) + NORM_EPS) * g_ref[...]


def _combine(h1, gates, dest_flat, y_slots, norm_g):
    T, D = h1.shape
    tm = 256
    n_tiles = T // tm
    return pl.pallas_call(
        _combine_kernel,
        grid=(n_tiles,),
        in_specs=[pl.BlockSpec((tm * TOP_K,), lambda i: (i,), memory_space=pltpu.SMEM),
                  pl.BlockSpec((tm * TOP_K,), lambda i: (jnp.minimum(i + 1, n_tiles - 1),),
                               memory_space=pltpu.SMEM),
                  pl.BlockSpec((tm, D), lambda i: (i, 0)),
                  pl.BlockSpec((tm, TOP_K), lambda i: (i, 0)),
                  pl.BlockSpec((1, D), lambda i: (0, 0)),
                  pl.BlockSpec(memory_space=pl.ANY)],
        out_specs=pl.BlockSpec((tm, D), lambda i: (i, 0)),
        out_shape=jax.ShapeDtypeStruct((T, D), F32),
        scratch_shapes=[pltpu.VMEM((2, TOP_K, tm * ROW_SUBLANES, LANES), F32),
                        pltpu.SemaphoreType.DMA((2,))],
        compiler_params=_cparams("arbitrary"),
        name="moe_combine",
    )(dest_flat, dest_flat, h1, gates, norm_g.reshape(1, D), y_slots)


def _moe(h1, xn2, idx, gates, counts, w_gate, b_gate, w_up, b_up, w_down, b_down, norm_final_g):
    T, D = h1.shape
    E = w_gate.shape[0]
    n_blocks = T * TOP_K // MOE_BLOCK + E
    cnt = counts[0].astype(jnp.int32)
    padded = (cnt + MOE_BLOCK - 1) // MOE_BLOCK * MOE_BLOCK
    pad_end = jnp.cumsum(padded)
    pad_start = (pad_end - padded).astype(F32).reshape(1, EXPERT_LANES)
    block_first = jnp.arange(n_blocks, dtype=jnp.int32) * MOE_BLOCK
    block_expert = jnp.minimum(
        jnp.sum((pad_end[None, :E] <= block_first[:, None]).astype(jnp.int32), axis=1), E - 1)
    n_used = (pad_end[E - 1] // MOE_BLOCK).astype(jnp.int32).reshape(1)
    last_block = jnp.where(padded[:E] > 0, pad_end[:E] // MOE_BLOCK - 1, -1)
    spare = n_used[0] + jnp.arange(E, dtype=jnp.int32)
    zero_blocks = jnp.concatenate([last_block, jnp.where(spare < n_blocks, spare, -1)]).astype(jnp.int32)
    dest = _route(idx, pad_start).reshape(T * TOP_K)
    slots = _scatter_rows(xn2, dest, zero_blocks, n_blocks * MOE_BLOCK)
    y_slots = _expert_ffn(slots, block_expert, n_used, w_gate, b_gate, w_up, b_up, w_down, b_down)
    return _combine(h1, gates, dest, y_slots, norm_final_g)


def kernel(x, norm_mix_g, w_in, attn_norm_g, rwkv_mu, rwkv_w0, rwkv_w2, rwkv_a0, rwkv_a2,
           rwkv_g2, rwkv_k_k, rwkv_k_a, rwkv_r_k, rwkv_ln_g, rwkv_ln_b, w_out, norm_ffn_g,
           w_router, b_router, moe_w_gate, moe_b_gate, moe_w_up, moe_b_up, moe_w_down,
           moe_b_down, norm_final_g):
    B, S, D = x.shape
    x2 = x.reshape(B * S, D)
    q, k, vt, km, xs = _in_projection(x2, norm_mix_g[0], w_in[0], rwkv_mu[0], B, S)
    o_att = _moba_attention(q, k, vt, km, attn_norm_g[0], B, S)
    o_rwkv = _rwkv_mix(xs, rwkv_w0[0], rwkv_w2[0], rwkv_a0[0], rwkv_a2[0], rwkv_g2[0], rwkv_k_k[0],
                       rwkv_k_a[0], rwkv_r_k[0], rwkv_ln_g[0], rwkv_ln_b[0], B, S)
    h1, xn2, idx, gates, counts = _outproj_router(o_att, o_rwkv, x2, w_out[0], norm_ffn_g[0],
                                                  w_router[0], b_router[0])
    out = _moe(h1, xn2, idx, gates, counts, moe_w_gate[0], moe_b_gate[0], moe_w_up[0], moe_b_up[0],
               moe_w_down[0], moe_b_down[0], norm_final_g)
    return out.reshape(B, S, D)
```

```python
import functools
import math

import jax
import jax.numpy as jnp
from jax import lax
from jax.experimental import pallas as pl
from jax.experimental.pallas import tpu as pltpu

F32 = jnp.float32
BF16 = jnp.bfloat16

HEAD_DIM = 64
N_HEADS = 8
GROUP_WIDTH = N_HEADS * HEAD_DIM
ATTN_TILE_WIDTH = N_HEADS * 2 * HEAD_DIM
V_ROWS = HEAD_DIM + 16
LOG2E = math.log2(math.e)
Q_SCALE = HEAD_DIM ** -0.5 * LOG2E
MOBA_BLOCK = 256
MOBA_TOPK = 3
N_EXPERTS = 32
TOP_K = 4
MOE_BLOCK = 256
SWIGLU_LIMIT = 7.0
SWIGLU_ALPHA = 1.702
NORM_EPS = 1e-6
NEG_INF = -1e30
RWKV_DECAY_SCALE = math.exp(-0.5)
RWKV_GN_EPS = HEAD_DIM * 1e-5
RWKV_LORA = (32, 32, 96)
RWKV_LORA_PAD = 256
RWKV_CHUNK = 64
RWKV_CHUNKS_PER_STEP = 4

VMEM_LIMIT = 48 * 1024 * 1024
EXPERT_VMEM_LIMIT = 56 * 1024 * 1024


def _cparams(*sem):
    return pltpu.CompilerParams(dimension_semantics=sem, vmem_limit_bytes=VMEM_LIMIT)


def _dot(a, b):
    return jnp.dot(a, b, preferred_element_type=F32)


def _dot_nt(a, b):
    return lax.dot_general(a, b, (((1,), (1,)), ((), ())), preferred_element_type=F32)


def _dot_tn(a, b):
    return lax.dot_general(a, b, (((0,), (0,)), ((), ())), preferred_element_type=F32)


def _split2(x):
    hi = x.astype(BF16)
    lo = (x - hi.astype(F32)).astype(BF16)
    return hi, lo


def _inproj_kernel(x_ref, g_ref, wq_ref, wk_ref, wvt_ref, wr_ref, mu_ref, qaug_ref, kaug_ref, vaug_ref,
                   q_ref, k_ref, vt_ref, kmean_ref, xs_ref, carry_ref, *, tiles_per_seq):
    i = pl.program_id(0)
    x = x_ref[...]
    xn = x * lax.rsqrt(jnp.mean(x * x, axis=-1, keepdims=True) + NORM_EPS) * g_ref[...]
    xb = xn.astype(BF16)
    q_ref[...] = (_dot(xb, wq_ref[...]) * Q_SCALE + qaug_ref[...]).astype(BF16)
    k = _dot(xb, wk_ref[...])
    k_ref[...] = (k + kaug_ref[...]).astype(BF16)
    kmean_ref[0] = jnp.mean(k, axis=0, keepdims=True)
    vt_ref[0] = (_dot_nt(wvt_ref[...], xb) + vaug_ref[...]).astype(BF16)
    pr = _dot(xb, wr_ref[...])

    @pl.when(i % tiles_per_seq == 0)
    def _():
        carry_ref[...] = jnp.zeros_like(carry_ref)

    tm = pr.shape[0]
    rolled = pltpu.roll(pr, 1, 0)
    row = lax.broadcasted_iota(jnp.int32, pr.shape, 0)
    prev = jnp.where(row == 0, carry_ref[...], rolled)
    carry_ref[...] = pr[tm - 1:tm, :]
    xs_ref[...] = pr + (prev - pr) * mu_ref[...]


def _in_projection(x2, norm_g, w_in, mu, batch, seq):
    T, D = x2.shape
    tm = MOBA_BLOCK
    nb = seq // tm
    GW = GROUP_WIDTH
    AW = ATTN_TILE_WIDTH

    def spread(w):
        w = w.reshape(D, N_HEADS, HEAD_DIM)
        return jnp.pad(w, ((0, 0), (0, 0), (0, HEAD_DIM))).reshape(D, AW).astype(BF16)

    wq = spread(w_in[:, :GW])
    wk = spread(w_in[:, GW:2 * GW])
    pos = jnp.arange(tm, dtype=F32)[:, None, None]
    slope = jnp.asarray(_alibi_slopes(), F32)[None, :, None]
    lane = jnp.arange(2 * HEAD_DIM)[None, None, :] - HEAD_DIM
    bias = slope * pos * LOG2E
    b_hi = bias.astype(BF16).astype(F32)
    b_lo = (bias - b_hi).astype(BF16).astype(F32)
    pick = lambda l0, l1, l2, l3: jnp.where(lane == 0, l0, jnp.where(lane == 1, l1, jnp.where(
        lane == 2, l2, jnp.where(lane == 3, l3, 0.0)))).reshape(tm, AW)
    qaug = pick(-b_hi, -b_lo, 1.0, 1.0)
    kaug = pick(1.0, 1.0, b_hi, b_lo)
    wvt = jnp.pad(w_in[:, 2 * GW:3 * GW].T.reshape(N_HEADS, HEAD_DIM, D),
                  ((0, 0), (0, V_ROWS - HEAD_DIM), (0, 0))).reshape(N_HEADS * V_ROWS, D).astype(BF16)
    vaug = jnp.broadcast_to((jnp.arange(N_HEADS * V_ROWS) % V_ROWS == HEAD_DIM).astype(F32)[:, None],
                            (N_HEADS * V_ROWS, tm))
    rw = 3 * GW + RWKV_LORA_PAD
    n_r = w_in.shape[1] - 3 * GW
    wr = jnp.pad(w_in[:, 3 * GW:], ((0, 0), (0, rw - n_r))).astype(BF16)
    mu_p = jnp.pad(mu, (0, rw - n_r)).reshape(1, rw)
    const = lambda i: (0, 0)
    return pl.pallas_call(
        functools.partial(_inproj_kernel, tiles_per_seq=nb),
        grid=(T // tm,),
        in_specs=[
            pl.BlockSpec((tm, D), lambda i: (i, 0)),
            pl.BlockSpec((1, D), const),
            pl.BlockSpec((D, AW), const),
            pl.BlockSpec((D, AW), const),
            pl.BlockSpec((N_HEADS * V_ROWS, D), const),
            pl.BlockSpec((D, rw), const),
            pl.BlockSpec((1, rw), const),
            pl.BlockSpec((tm, AW), const),
            pl.BlockSpec((tm, AW), const),
            pl.BlockSpec((N_HEADS * V_ROWS, tm), const),
        ],
        out_specs=[
            pl.BlockSpec((tm, AW), lambda i: (i, 0)),
            pl.BlockSpec((tm, AW), lambda i: (i, 0)),
            pl.BlockSpec((1, N_HEADS * V_ROWS, tm), lambda i: (i, 0, 0)),
            pl.BlockSpec((1, 1, AW), lambda i: (i, 0, 0)),
            pl.BlockSpec((tm, rw), lambda i: (i, 0)),
        ],
        out_shape=[
            jax.ShapeDtypeStruct((T, AW), BF16),
            jax.ShapeDtypeStruct((T, AW), BF16),
            jax.ShapeDtypeStruct((T // tm, N_HEADS * V_ROWS, tm), BF16),
            jax.ShapeDtypeStruct((T // tm, 1, AW), F32),
            jax.ShapeDtypeStruct((T, rw), F32),
        ],
        scratch_shapes=[pltpu.VMEM((1, rw), F32)],
        compiler_params=_cparams("arbitrary"),
        name="in_projection",
    )(x2, norm_g.reshape(1, D), wq, wk, wvt, wr, mu_p, qaug, kaug, vaug)


def _moba_kernel(q_ref, k_ref, vt_ref, kmean_ref, g_ref,
                 o_ref, selb_ref, m_ref, acc_ref, ot_ref, *, nb, nbp, slopes):
    i = pl.program_id(1)
    blk = MOBA_BLOCK
    heads = range(N_HEADS)
    tile = [slice(h * 2 * HEAD_DIM, (h + 1) * 2 * HEAD_DIM) for h in heads]
    vrow = [slice(h * V_ROWS, (h + 1) * V_ROWS) for h in heads]
    n_iota = lax.broadcasted_iota(jnp.int32, (nbp, blk), 0)
    valid = n_iota < i

    gates = []
    for h in heads:
        km_hi, km_lo = _split2(kmean_ref[0, :, tile[h]])
        qh = q_ref[:, tile[h]]
        gates.append(jnp.where(valid, _dot_nt(km_hi, qh) + _dot_nt(km_lo, qh), NEG_INF))
    for h in heads:
        rank = jnp.zeros((nbp, blk), jnp.int32)
        for m in range(nb):
            gm = gates[h][m:m + 1, :]
            tie = (n_iota > m).astype(jnp.int32)
            rank = rank + jnp.where(gm > gates[h], 1, jnp.where(gm == gates[h], tie, 0))
        selb_ref[h] = jnp.where(rank < MOBA_TOPK, jnp.where(valid, 0.0, NEG_INF), NEG_INF)

    own = pl.ds(pl.multiple_of(i * blk, blk), blk)
    future = (lax.broadcasted_iota(jnp.int32, (blk, blk), 0) >
              lax.broadcasted_iota(jnp.int32, (blk, blk), 1))
    st = [jnp.where(future, NEG_INF, _dot_nt(k_ref[0, own, tile[h]], q_ref[:, tile[h]])) for h in heads]
    ps = []
    for h in heads:
        m0 = jnp.max(st[h], axis=0, keepdims=True)
        m_ref[h] = m0
        ps.append(jnp.exp2(st[h] - m0).astype(BF16))
    for h in heads:
        acc_ref[vrow[h], :] = _dot(vt_ref[0, i, vrow[h], :], ps[h])

    def body(j, _):
        rows = pl.ds(pl.multiple_of(j * blk, blk), blk)
        dist = (i - j).astype(F32) * (blk * LOG2E)
        st = [_dot_nt(k_ref[0, rows, tile[h]], q_ref[:, tile[h]]) for h in heads]
        ps, alphas = [], []
        for h in heads:
            off = selb_ref[h, pl.ds(j, 1), :] - slopes[h] * dist
            m_old = m_ref[h]
            m_new = jnp.maximum(m_old, jnp.max(st[h], axis=0, keepdims=True) + off)
            m_ref[h] = m_new
            ps.append(jnp.exp2(st[h] - (m_new - off)).astype(BF16))
            alphas.append(jnp.exp2(m_old - m_new))
        for h in heads:
            acc_ref[vrow[h], :] = alphas[h] * acc_ref[vrow[h], :] + _dot(vt_ref[0, j, vrow[h], :], ps[h])
        return 0

    lax.fori_loop(0, i, body, 0)

    for h in heads:
        v0 = h * V_ROWS
        o = acc_ref[v0:v0 + HEAD_DIM, :] / acc_ref[v0 + HEAD_DIM:v0 + HEAD_DIM + 1, :]
        ot_ref[h * HEAD_DIM:(h + 1) * HEAD_DIM, :] = o * lax.rsqrt(
            jnp.mean(o * o, axis=0, keepdims=True) + NORM_EPS)
    o_ref[...] = (ot_ref[...].T * g_ref[...]).astype(o_ref.dtype)


def _alibi_slopes():
    return [2.0 ** (-8.0 * (h + 1) / N_HEADS) for h in range(N_HEADS)]


def _moba_attention(q, k, vt, kmean, attn_norm_g, batch, seq):
    GW = GROUP_WIDTH
    AW = ATTN_TILE_WIDTH
    blk = MOBA_BLOCK
    nb = seq // blk
    nbp = -(-nb // 16) * 16
    k3 = k.reshape(batch, seq, AW)
    VW = N_HEADS * V_ROWS
    vt4 = vt.reshape(batch, nb, VW, blk)
    km = jnp.pad(kmean.reshape(batch, nb, AW), ((0, 0), (0, nbp - nb), (0, 0)))
    return pl.pallas_call(
        functools.partial(_moba_kernel, nb=nb, nbp=nbp, slopes=_alibi_slopes()),
        grid=(batch, nb),
        in_specs=[
            pl.BlockSpec((blk, AW), lambda b, i: (b * nb + i, 0)),
            pl.BlockSpec((1, seq, AW), lambda b, i: (b, 0, 0)),
            pl.BlockSpec((1, nb, VW, blk), lambda b, i: (b, 0, 0, 0)),
            pl.BlockSpec((1, nbp, AW), lambda b, i: (b, 0, 0)),
            pl.BlockSpec((1, GW), lambda b, i: (0, 0)),
        ],
        out_specs=pl.BlockSpec((blk, GW), lambda b, i: (b * nb + i, 0)),
        out_shape=jax.ShapeDtypeStruct((batch * seq, GW), BF16),
        scratch_shapes=[pltpu.VMEM((N_HEADS, nbp, blk), F32),
                        pltpu.VMEM((N_HEADS, 1, blk), F32),
                        pltpu.VMEM((VW, blk), F32),
                        pltpu.VMEM((GW, blk), F32)],
        compiler_params=_cparams("arbitrary", "arbitrary"),
        name="moba_attention",
    )(q, k3, vt4, km, attn_norm_g.reshape(1, GW))


def _mm(a, b, dims=((1,), (0,)), passes=3):
    dn = (dims, ((), ()))
    dg = lambda u, v: lax.dot_general(u, v, dn, preferred_element_type=F32)
    if passes == 1:
        return dg(a.astype(BF16), b.astype(BF16))
    a_hi, a_lo = _split2(a)
    b_hi, b_lo = _split2(b)
    return dg(a_hi, b_hi) + dg(a_hi, b_lo) + dg(a_lo, b_hi)


_NT = ((1,), (1,))
_TN = ((0,), (0,))
RWKV_PASSES = {"scores": 1, "inverse": 1, "outer": 1, "state": 1}


def _head_sum(x, pair_ones):
    x_hi, x_lo = _split2(x)
    w = pair_ones.shape[0]
    tiles = [_dot(x_hi[:, t:t + w], pair_ones) + _dot(x_lo[:, t:t + w], pair_ones)
             for t in range(0, x.shape[1], w)]
    return jnp.concatenate(tiles, axis=1)


def _rwkv_kernel(xs_ref, w2_ref, a2_ref, g2_ref, vec_ref, tril_ref, hsum_ref,
                 o_ref, state_ref, cum_ref, lw_ref, r_ref, kk_ref, b_ref, k2_ref, v_ref, oc_ref):
    GW = GROUP_WIDTH
    L = RWKV_CHUNK
    N = HEAD_DIM
    tm = xs_ref.shape[0]
    w0, a0, k_k, k_a, r_k, ln_g, ln_b = [vec_ref[n:n + 1, :] for n in range(7)]

    @pl.when(pl.program_id(1) == 0)
    def _():
        state_ref[...] = jnp.zeros_like(state_ref)

    r = xs_ref[:, 0:GW]
    k = xs_ref[:, GW:2 * GW]
    v = xs_ref[:, 2 * GW:3 * GW]
    lo = xs_ref[:, 3 * GW:]
    hsum = hsum_ref[...]
    logw = -RWKV_DECAY_SCALE * jax.nn.sigmoid(w0 + _dot(jnp.tanh(lo).astype(BF16), w2_ref[...]))
    a = jax.nn.sigmoid(a0 + _dot(lo.astype(BF16), a2_ref[...]))
    gate = _dot(jax.nn.sigmoid(lo).astype(BF16), g2_ref[...])
    kk = k * k_k
    kk = kk * lax.rsqrt(jnp.maximum(_head_sum(kk * kk, hsum), 1e-24))
    k2 = k * (1.0 + (a - 1.0) * k_a)
    bonus = _head_sum(r * k2 * r_k, hsum) * v
    lw_hi, lw_lo = _split2(logw)
    cum_ref[...] = _dot(tril_ref[...], lw_hi) + _dot(tril_ref[...], lw_lo)
    lw_ref[...] = logw
    r_ref[...] = r
    kk_ref[...] = kk
    b_ref[...] = kk * a
    k2_ref[...] = k2
    v_ref[...] = v

    row = lax.broadcasted_iota(jnp.int32, (L, L), 0)
    col = lax.broadcasted_iota(jnp.int32, (L, L), 1)
    strict = row > col
    incl = row >= col
    eye = row == col

    U = RWKV_CHUNKS_PER_STEP
    hsl = [slice(h * N, (h + 1) * N) for h in range(N_HEADS)]
    units = [(u, h) for u in range(U) for h in range(N_HEADS)]

    def step(c, _):
        pre = []
        for u in range(U):
            rows = pl.ds(pl.multiple_of((c * U + u) * L, L), L)
            cum = cum_ref[rows, :]
            cum_end = cum[L - 1:L, :]
            w_incl = jnp.exp(cum)
            w_excl = jnp.exp(cum - lw_ref[rows, :])
            w_inv = jnp.exp(-cum)
            e_end = jnp.exp(cum_end - cum)
            pre.append(dict(
                rows=rows, w_end=jnp.exp(cum_end),
                rhat=r_ref[rows, :] * w_incl, kkhat=kk_ref[rows, :] * w_excl,
                btil=b_ref[rows, :] * w_inv, ktil=k2_ref[rows, :] * w_inv,
                bbar=b_ref[rows, :] * e_end, kbar=k2_ref[rows, :] * e_end, v=v_ref[rows, :]))
        sl = lambda name, u, h: pre[u][name][:, hsl[h]]
        ma = [_mm(jnp.concatenate([sl("kkhat", u, h), sl("rhat", u, h)], axis=0),
                  jnp.concatenate([sl("btil", u, h), sl("ktil", u, h)], axis=0), _NT, RWKV_PASSES["scores"])
              for u, h in units]
        m_bk = [jnp.where(strict, m[:L, :L], 0.0) for m in ma]
        m_kk = [jnp.where(strict, m[:L, L:], 0.0) for m in ma]
        a_br = [jnp.where(incl, m[L:, :L], 0.0) for m in ma]
        a_kr = [jnp.where(incl, m[L:, L:], 0.0) for m in ma]
        n_u = range(len(units))
        vh = [sl("v", u, h) for u, h in units]
        pi = RWKV_PASSES["inverse"]
        y = [jnp.concatenate([sl("kkhat", *units[n]), _mm(m_kk[n], vh[n], passes=pi)], axis=1) for n in n_u]
        y = [y[n] - _mm(m_bk[n], y[n], passes=pi) for n in n_u]
        pw = m_bk
        for _ in range(int(math.log2(L)) - 1):
            pw = [_mm(pw[n], pw[n], passes=pi) for n in n_u]
            y = [y[n] + _mm(pw[n], y[n], passes=pi) for n in n_u]
        po = RWKV_PASSES["outer"]
        xtb = [_mm(y[n], sl("bbar", *units[n]), _TN, po) for n in n_u]
        vtk = [_mm(vh[n], sl("kbar", *units[n]), _TN, po) for n in n_u]
        ax = [_mm(a_br[n], y[n], passes=po) for n in n_u]
        akv = [_mm(a_kr[n], vh[n], passes=po) for n in n_u]
        ps = RWKV_PASSES["state"]
        state = [state_ref[h] for h in range(N_HEADS)]
        for n, (u, h) in enumerate(units):
            r2 = sl("rhat", u, h) - ax[n][:, :N]
            oc_ref[pre[u]["rows"], hsl[h]] = _mm(r2, state[h], _NT, ps) + akv[n] - ax[n][:, N:]
            p_mat = jnp.where(eye, pre[u]["w_end"][:, hsl[h]], 0.0) - xtb[n][:N]
            state[h] = _mm(state[h], p_mat, passes=ps) + vtk[n] - xtb[n][N:]
        for h in range(N_HEADS):
            state_ref[h] = state[h]
        return 0

    lax.fori_loop(0, tm // (L * U), step, 0)

    o = oc_ref[...]
    mean = _head_sum(o, hsum) * (1.0 / N)
    d = o - mean
    var = _head_sum(d * d, hsum) * (1.0 / N)
    o = d * lax.rsqrt(var + RWKV_GN_EPS) * ln_g + ln_b
    o_ref[...] = ((o + bonus) * gate).astype(o_ref.dtype)


def _rwkv_mix(xs, w0, w2, a0, a2, g2, k_k, k_a, r_k, ln_g, ln_b, batch, seq):
    T, rw = xs.shape
    GW = GROUP_WIDTH
    L = RWKV_CHUNK
    tm = 256
    nl = rw - 3 * GW
    d0, d1, d2 = RWKV_LORA
    w2p = jnp.pad(w2, ((0, nl - d0), (0, 0))).astype(BF16)
    a2p = jnp.pad(a2, ((d0, nl - d0 - d1), (0, 0))).astype(BF16)
    g2p = jnp.pad(g2, ((d0 + d1, nl - d0 - d1 - d2), (0, 0))).astype(BF16)
    vecs = jnp.stack([w0, a0, k_k, k_a, r_k.reshape(GW), ln_g, ln_b, jnp.zeros((GW,), F32)])
    t_idx = jnp.arange(tm)
    tril = ((t_idx[:, None] >= t_idx[None, :]) & (t_idx[:, None] // L == t_idx[None, :] // L)).astype(BF16)
    c_idx = jnp.arange(2 * HEAD_DIM) // HEAD_DIM
    hsum = (c_idx[:, None] == c_idx[None, :]).astype(BF16)
    tps = seq // tm
    const = lambda b, i: (0, 0)
    scr = lambda: pltpu.VMEM((tm, GW), F32)
    return pl.pallas_call(
        _rwkv_kernel,
        grid=(batch, tps),
        in_specs=[
            pl.BlockSpec((tm, rw), lambda b, i: (b * tps + i, 0)),
            pl.BlockSpec((nl, GW), const),
            pl.BlockSpec((nl, GW), const),
            pl.BlockSpec((nl, GW), const),
            pl.BlockSpec((8, GW), const),
            pl.BlockSpec((tm, tm), const),
            pl.BlockSpec((2 * HEAD_DIM, 2 * HEAD_DIM), const),
        ],
        out_specs=pl.BlockSpec((tm, GW), lambda b, i: (b * tps + i, 0)),
        out_shape=jax.ShapeDtypeStruct((T, GW), BF16),
        scratch_shapes=[pltpu.VMEM((N_HEADS, HEAD_DIM, HEAD_DIM), F32)] + [scr() for _ in range(8)],
        compiler_params=_cparams("arbitrary", "arbitrary"),
        name="rwkv7_mix",
    )(xs, w2p, a2p, g2p, vecs, tril, hsum)


EXPERT_LANES = 128

ROW_SUBLANES = 8
LANES = 128


def _store_row_tiles(ref, x):
    n = x.shape[0]
    for s in range(ROW_SUBLANES):
        ref[pl.ds(s, n, stride=ROW_SUBLANES), :] = x[:, s * LANES:(s + 1) * LANES]


def _load_row_tiles(ref, n):
    return jnp.concatenate([ref[pl.ds(s, n, stride=ROW_SUBLANES), :] for s in range(ROW_SUBLANES)], axis=1)


def _outproj_router_kernel(oa_ref, or_ref, x_ref, wa_ref, wr_ref, g_ref, wrt_ref, brt_ref,
                           h_ref, xn_ref, idx_ref, gate_ref, cnt_ref):
    h = x_ref[...] + _dot(oa_ref[...], wa_ref[...]) + _dot(or_ref[...], wr_ref[...])
    h_ref[...] = h
    xn = h * lax.rsqrt(jnp.mean(h * h, axis=-1, keepdims=True) + NORM_EPS) * g_ref[...]
    _store_row_tiles(xn_ref, xn)
    logits = _mm(xn, wrt_ref[...]) + brt_ref[...]
    tm = logits.shape[0]
    lane = lax.broadcasted_iota(jnp.int32, logits.shape, 1)
    lane4 = lax.broadcasted_iota(jnp.int32, (tm, TOP_K), 1)
    idx_out = jnp.zeros((tm, TOP_K), jnp.int32)
    val_out = jnp.zeros((tm, TOP_K), F32)
    member = jnp.zeros(logits.shape, F32)
    top = None
    denom = jnp.zeros((tm, 1), F32)
    for kk in range(TOP_K):
        mx = jnp.max(logits, axis=-1, keepdims=True)
        idx = jnp.min(jnp.where(logits == mx, lane, EXPERT_LANES), axis=-1, keepdims=True)
        hit = lane == idx
        member = jnp.where(hit, 1.0, member)
        logits = jnp.where(hit, -jnp.inf, logits)
        top = mx if top is None else top
        e = jnp.exp(mx - top)
        denom = denom + e
        idx_out = jnp.where(lane4 == kk, idx, idx_out)
        val_out = jnp.where(lane4 == kk, e, val_out)
    idx_ref[...] = idx_out
    gate_ref[...] = val_out / denom

    @pl.when(pl.program_id(0) == 0)
    def _():
        cnt_ref[...] = jnp.zeros_like(cnt_ref)

    cnt_ref[...] += jnp.sum(member, axis=0, keepdims=True)


def _outproj_router(o_att, o_rwkv, x2, w_out, norm_g, w_router, b_router):
    T, D = x2.shape
    GW = GROUP_WIDTH
    tm = 512
    E = w_router.shape[1]
    wa = w_out[:GW].astype(BF16)
    wr = w_out[GW:].astype(BF16)
    wrt = jnp.pad(w_router, ((0, 0), (0, EXPERT_LANES - E)))
    brt = jnp.pad(b_router, (0, EXPERT_LANES - E), constant_values=NEG_INF).reshape(1, EXPERT_LANES)
    const = lambda i: (0, 0)
    tile = lambda w: pl.BlockSpec((tm, w), lambda i: (i, 0))
    return pl.pallas_call(
        _outproj_router_kernel,
        grid=(T // tm,),
        in_specs=[tile(GW), tile(GW), tile(D),
                  pl.BlockSpec((GW, D), const), pl.BlockSpec((GW, D), const),
                  pl.BlockSpec((1, D), const), pl.BlockSpec((D, EXPERT_LANES), const),
                  pl.BlockSpec((1, EXPERT_LANES), const)],
        out_specs=[tile(D), pl.BlockSpec((tm * ROW_SUBLANES, LANES), lambda i: (i, 0)),
                   tile(TOP_K), tile(TOP_K), pl.BlockSpec((1, EXPERT_LANES), const)],
        out_shape=[jax.ShapeDtypeStruct((T, D), F32), jax.ShapeDtypeStruct((T * ROW_SUBLANES, LANES), F32),
                   jax.ShapeDtypeStruct((T, TOP_K), jnp.int32), jax.ShapeDtypeStruct((T, TOP_K), F32),
                   jax.ShapeDtypeStruct((1, EXPERT_LANES), F32)],
        compiler_params=_cparams("arbitrary"),
        name="outproj_router",
    )(o_att, o_rwkv, x2, wa, wr, norm_g.reshape(1, D), wrt, brt)


def _route_kernel(idx_ref, start_ref, tril_ref, dest_ref, carry_ref):
    @pl.when(pl.program_id(0) == 0)
    def _():
        carry_ref[...] = jnp.zeros_like(carry_ref)

    idx = idx_ref[...]
    tm = idx.shape[0]
    lane = lax.broadcasted_iota(jnp.int32, (tm, EXPERT_LANES), 1)
    lane4 = lax.broadcasted_iota(jnp.int32, (tm, TOP_K), 1)
    hits = [lane == idx[:, kk:kk + 1] for kk in range(TOP_K)]
    member = sum(jnp.where(hh, 1.0, 0.0) for hh in hits)
    before = _dot(tril_ref[...], member.astype(BF16)) + carry_ref[...]
    base = before + start_ref[...]
    dest = jnp.zeros((tm, TOP_K), jnp.int32)
    for kk in range(TOP_K):
        d = jnp.sum(jnp.where(hits[kk], base, 0.0), axis=-1, keepdims=True)
        dest = jnp.where(lane4 == kk, d.astype(jnp.int32), dest)
    dest_ref[...] = dest
    carry_ref[...] += jnp.sum(member, axis=0, keepdims=True)


def _route(idx, pad_start):
    T = idx.shape[0]
    tm = 512
    t_idx = jnp.arange(tm)
    tril = (t_idx[:, None] > t_idx[None, :]).astype(BF16)
    const = lambda i: (0, 0)
    return pl.pallas_call(
        _route_kernel,
        grid=(T // tm,),
        in_specs=[pl.BlockSpec((tm, TOP_K), lambda i: (i, 0)),
                  pl.BlockSpec((1, EXPERT_LANES), const),
                  pl.BlockSpec((tm, tm), const)],
        out_specs=pl.BlockSpec((tm, TOP_K), lambda i: (i, 0)),
        out_shape=jax.ShapeDtypeStruct((T, TOP_K), jnp.int32),
        scratch_shapes=[pltpu.VMEM((1, EXPERT_LANES), F32)],
        compiler_params=_cparams("arbitrary"),
        name="moe_route",
    )(idx, pad_start, tril)


DMA_ROWS_PER_STEP = 4
BLOCK_TILE_ROWS = MOE_BLOCK * ROW_SUBLANES


def _start_row_copies(make_copy, n_rows):
    def issue(g, _):
        for u in range(DMA_ROWS_PER_STEP):
            for kk in range(TOP_K):
                make_copy(g * DMA_ROWS_PER_STEP + u, kk).start(priority=kk % 2)
        return 0

    lax.fori_loop(0, n_rows // DMA_ROWS_PER_STEP, issue, 0)


def _wait_row_copies(make_copy, n_rows):
    def drain(g, _):
        for u in range(DMA_ROWS_PER_STEP):
            for kk in range(TOP_K):
                make_copy(g * DMA_ROWS_PER_STEP + u, kk).wait()
        return 0

    lax.fori_loop(0, n_rows // DMA_ROWS_PER_STEP, drain, 0)


def _row_tile(ref, r):
    return ref.at[pl.ds(pl.multiple_of(r * ROW_SUBLANES, ROW_SUBLANES), ROW_SUBLANES), :]


def _scatter_kernel(zero_ref, dest_ref, x_ref, slots_ref, zbuf_ref, sem, zsem):
    @pl.when(pl.program_id(0) == 0)
    def _():
        zbuf_ref[...] = jnp.zeros_like(zbuf_ref)

        def zero_copy(z):
            first = pl.multiple_of(zero_ref[z] * BLOCK_TILE_ROWS, BLOCK_TILE_ROWS)
            return pltpu.make_async_copy(zbuf_ref, slots_ref.at[pl.ds(first, BLOCK_TILE_ROWS), :], zsem)

        for z in range(zero_ref.shape[0]):
            @pl.when(zero_ref[z] >= 0)
            def _():
                zero_copy(z).start()
        for z in range(zero_ref.shape[0]):
            @pl.when(zero_ref[z] >= 0)
            def _():
                zero_copy(z).wait()

    def row_copy(t, kk):
        return pltpu.make_async_copy(_row_tile(x_ref, t), _row_tile(slots_ref, dest_ref[t * TOP_K + kk]), sem)

    n_rows = x_ref.shape[0] // ROW_SUBLANES
    _start_row_copies(row_copy, n_rows)
    _wait_row_copies(row_copy, n_rows)


def _scatter_rows(xn_tiles, dest_flat, zero_blocks, n_slots):
    T = xn_tiles.shape[0] // ROW_SUBLANES
    tm = 256
    grid_spec = pltpu.PrefetchScalarGridSpec(
        num_scalar_prefetch=1,
        grid=(T // tm,),
        in_specs=[pl.BlockSpec((tm * TOP_K,), lambda i, z: (i,), memory_space=pltpu.SMEM),
                  pl.BlockSpec((tm * ROW_SUBLANES, LANES), lambda i, z: (i, 0))],
        out_specs=pl.BlockSpec(memory_space=pl.ANY),
        scratch_shapes=[pltpu.VMEM((BLOCK_TILE_ROWS, LANES), F32),
                        pltpu.SemaphoreType.DMA(()), pltpu.SemaphoreType.DMA(())],
    )
    return pl.pallas_call(
        _scatter_kernel,
        grid_spec=grid_spec,
        out_shape=jax.ShapeDtypeStruct((n_slots * ROW_SUBLANES, LANES), F32),
        compiler_params=_cparams("arbitrary"),
        name="moe_scatter",
    )(zero_blocks, dest_flat, xn_tiles)


def _expert_kernel(be_ref, nu_ref, x_ref, wg_ref, bg_ref, wu_ref, bu_ref, wd_ref, bd_ref, y_ref,
                   wg_bf, wu_bf, wd_bf):
    i = pl.program_id(0)
    used = i < nu_ref[0]

    @pl.when(jnp.logical_not(used))
    def _():
        y_ref[...] = jnp.zeros_like(y_ref)

    @pl.when(jnp.logical_and(used, jnp.logical_or(i == 0, be_ref[i] != be_ref[jnp.maximum(i - 1, 0)])))
    def _():
        wg_bf[...] = wg_ref[0].astype(BF16)
        wu_bf[...] = wu_ref[0].astype(BF16)
        wd_bf[...] = wd_ref[0].astype(BF16)

    @pl.when(used)
    def _():
        xb = _load_row_tiles(x_ref, MOE_BLOCK).astype(BF16)
        gate = jnp.minimum(_dot(xb, wg_bf[...]) + bg_ref[0], SWIGLU_LIMIT)
        up = jnp.clip(_dot(xb, wu_bf[...]) + bu_ref[0], -SWIGLU_LIMIT, SWIGLU_LIMIT)
        glu = gate * jax.nn.sigmoid(SWIGLU_ALPHA * gate)
        _store_row_tiles(y_ref, _dot(((up + 1.0) * glu).astype(BF16), wd_bf[...]) + bd_ref[0])


def _expert_ffn(slots, block_expert, n_used, w_gate, b_gate, w_up, b_up, w_down, b_down):
    E, D, F = w_gate.shape
    n_blocks = slots.shape[0] // BLOCK_TILE_ROWS
    row = lambda i, be, nu: (jnp.minimum(i, nu[0] - 1), 0)
    wsel = lambda i, be, nu: (be[i], 0, 0)
    grid_spec = pltpu.PrefetchScalarGridSpec(
        num_scalar_prefetch=2,
        grid=(n_blocks,),
        in_specs=[pl.BlockSpec((BLOCK_TILE_ROWS, LANES), row),
                  pl.BlockSpec((1, D, F), wsel), pl.BlockSpec((1, 1, F), wsel),
                  pl.BlockSpec((1, D, F), wsel), pl.BlockSpec((1, 1, F), wsel),
                  pl.BlockSpec((1, F, D), wsel), pl.BlockSpec((1, 1, D), wsel)],
        out_specs=pl.BlockSpec((BLOCK_TILE_ROWS, LANES), lambda i, be, nu: (i, 0)),
        scratch_shapes=[pltpu.VMEM((D, F), BF16), pltpu.VMEM((D, F), BF16), pltpu.VMEM((F, D), BF16)],
    )
    return pl.pallas_call(
        _expert_kernel,
        grid_spec=grid_spec,
        out_shape=jax.ShapeDtypeStruct(slots.shape, F32),
        compiler_params=pltpu.CompilerParams(dimension_semantics=("arbitrary",),
                                             vmem_limit_bytes=EXPERT_VMEM_LIMIT),
        name="moe_experts",
    )(block_expert, n_used, slots, w_gate, b_gate.reshape(E, 1, F), w_up,
      b_up.reshape(E, 1, F), w_down, b_down.reshape(E, 1, D))


def _combine_kernel(dest_ref, dest_next_ref, h_ref, gate_ref, g_ref, y_ref, o_ref, buf_ref, sem):
    i = pl.program_id(0)
    tm = h_ref.shape[0]
    slot = i % 2

    def gather(dref, s):
        def row_copy(t, kk):
            return pltpu.make_async_copy(_row_tile(y_ref, dref[t * TOP_K + kk]),
                                         _row_tile(buf_ref.at[s, kk], t), sem.at[s])
        return row_copy

    @pl.when(i == 0)
    def _():
        _start_row_copies(gather(dest_ref, 0), tm)

    @pl.when(i + 1 < pl.num_programs(0))
    def _():
        _start_row_copies(gather(dest_next_ref, 1 - slot), tm)

    _wait_row_copies(gather(dest_ref, slot), tm)
    gates = gate_ref[...]
    h = h_ref[...]
    for kk in range(TOP_K):
        h = h + gates[:, kk:kk + 1] * _load_row_tiles(buf_ref.at[slot, kk], tm)
    o_ref[...] = h * lax.rsqrt(jnp.mean(h * h, axis=-1, keepdims=True) + NORM_EPS) * g_ref[...]


def _combine(h1, gates, dest_flat, y_slots, norm_g):
    T, D = h1.shape
    tm = 256
    n_tiles = T // tm
    return pl.pallas_call(
        _combine_kernel,
        grid=(n_tiles,),
        in_specs=[pl.BlockSpec((tm * TOP_K,), lambda i: (i,), memory_space=pltpu.SMEM),
                  pl.BlockSpec((tm * TOP_K,), lambda i: (jnp.minimum(i + 1, n_tiles - 1),),
                               memory_space=pltpu.SMEM),
                  pl.BlockSpec((tm, D), lambda i: (i, 0)),
                  pl.BlockSpec((tm, TOP_K), lambda i: (i, 0)),
                  pl.BlockSpec((1, D), lambda i: (0, 0)),
                  pl.BlockSpec(memory_space=pl.ANY)],
        out_specs=pl.BlockSpec((tm, D), lambda i: (i, 0)),
        out_shape=jax.ShapeDtypeStruct((T, D), F32),
        scratch_shapes=[pltpu.VMEM((2, TOP_K, tm * ROW_SUBLANES, LANES), F32),
                        pltpu.SemaphoreType.DMA((2,))],
        compiler_params=_cparams("arbitrary"),
        name="moe_combine",
    )(dest_flat, dest_flat, h1, gates, norm_g.reshape(1, D), y_slots)


def _moe(h1, xn2, idx, gates, counts, w_gate, b_gate, w_up, b_up, w_down, b_down, norm_final_g):
    T, D = h1.shape
    E = w_gate.shape[0]
    n_blocks = T * TOP_K // MOE_BLOCK + E
    cnt = counts[0].astype(jnp.int32)
    padded = (cnt + MOE_BLOCK - 1) // MOE_BLOCK * MOE_BLOCK
    pad_end = jnp.cumsum(padded)
    pad_start = (pad_end - padded).astype(F32).reshape(1, EXPERT_LANES)
    block_first = jnp.arange(n_blocks, dtype=jnp.int32) * MOE_BLOCK
    block_expert = jnp.minimum(
        jnp.sum((pad_end[None, :E] <= block_first[:, None]).astype(jnp.int32), axis=1), E - 1)
    n_used = (pad_end[E - 1] // MOE_BLOCK).astype(jnp.int32).reshape(1)
    last_block = jnp.where(padded[:E] > 0, pad_end[:E] // MOE_BLOCK - 1, -1)
    spare = n_used[0] + jnp.arange(E, dtype=jnp.int32)
    zero_blocks = jnp.concatenate([last_block, jnp.where(spare < n_blocks, spare, -1)]).astype(jnp.int32)
    dest = _route(idx, pad_start).reshape(T * TOP_K)
    slots = _scatter_rows(xn2, dest, zero_blocks, n_blocks * MOE_BLOCK)
    y_slots = _expert_ffn(slots, block_expert, n_used, w_gate, b_gate, w_up, b_up, w_down, b_down)
    return _combine(h1, gates, dest, y_slots, norm_final_g)


def kernel(x, norm_mix_g, w_in, attn_norm_g, rwkv_mu, rwkv_w0, rwkv_w2, rwkv_a0, rwkv_a2,
           rwkv_g2, rwkv_k_k, rwkv_k_a, rwkv_r_k, rwkv_ln_g, rwkv_ln_b, w_out, norm_ffn_g,
           w_router, b_router, moe_w_gate, moe_b_gate, moe_w_up, moe_b_up, moe_w_down,
           moe_b_down, norm_final_g):
    B, S, D = x.shape
    x2 = x.reshape(B * S, D)
    q, k, vt, km, xs = _in_projection(x2, norm_mix_g[0], w_in[0], rwkv_mu[0], B, S)
    o_att = _moba_attention(q, k, vt, km, attn_norm_g[0], B, S)
    o_rwkv = _rwkv_mix(xs, rwkv_w0[0], rwkv_w2[0], rwkv_a0[0], rwkv_a2[0], rwkv_g2[0], rwkv_k_k[0],
                       rwkv_k_a[0], rwkv_r_k[0], rwkv_ln_g[0], rwkv_ln_b[0], B, S)
    h1, xn2, idx, gates, counts = _outproj_router(o_att, o_rwkv, x2, w_out[0], norm_ffn_g[0],
                                                  w_router[0], b_router[0])
    out = _moe(h1, xn2, idx, gates, counts, moe_w_gate[0], moe_b_gate[0], moe_w_up[0], moe_b_up[0],
               moe_w_down[0], moe_b_down[0], norm_final_g)
    return out.reshape(B, S, D)
```

```python
import functools
import math

import jax
import jax.numpy as jnp
from jax import lax
from jax.experimental import pallas as pl
from jax.experimental.pallas import tpu as pltpu

F32 = jnp.float32
BF16 = jnp.bfloat16

HEAD_DIM = 64
N_HEADS = 8
GROUP_WIDTH = N_HEADS * HEAD_DIM
ATTN_TILE_WIDTH = N_HEADS * 2 * HEAD_DIM
V_ROWS = HEAD_DIM + 16
LOG2E = math.log2(math.e)
Q_SCALE = HEAD_DIM ** -0.5 * LOG2E
MOBA_BLOCK = 256
MOBA_TOPK = 3
N_EXPERTS = 32
TOP_K = 4
MOE_BLOCK = 256
SWIGLU_LIMIT = 7.0
SWIGLU_ALPHA = 1.702
NORM_EPS = 1e-6
NEG_INF = -1e30
RWKV_DECAY_SCALE = math.exp(-0.5)
RWKV_GN_EPS = HEAD_DIM * 1e-5
RWKV_LORA = (32, 32, 96)
RWKV_LORA_PAD = 256
RWKV_CHUNK = 64
RWKV_CHUNKS_PER_STEP = 4

VMEM_LIMIT = 48 * 1024 * 1024
EXPERT_VMEM_LIMIT = 56 * 1024 * 1024


def _cparams(*sem):
    return pltpu.CompilerParams(dimension_semantics=sem, vmem_limit_bytes=VMEM_LIMIT)


def _dot(a, b):
    return jnp.dot(a, b, preferred_element_type=F32)


def _dot_nt(a, b):
    return lax.dot_general(a, b, (((1,), (1,)), ((), ())), preferred_element_type=F32)


def _dot_tn(a, b):
    return lax.dot_general(a, b, (((0,), (0,)), ((), ())), preferred_element_type=F32)


def _split2(x):
    hi = x.astype(BF16)
    lo = (x - hi.astype(F32)).astype(BF16)
    return hi, lo


def _inproj_kernel(x_ref, g_ref, wq_ref, wk_ref, wvt_ref, wr_ref, mu_ref, qaug_ref, kaug_ref, vaug_ref,
                   q_ref, k_ref, vt_ref, kmean_ref, xs_ref, carry_ref, *, tiles_per_seq):
    i = pl.program_id(0)
    x = x_ref[...]
    xn = x * lax.rsqrt(jnp.mean(x * x, axis=-1, keepdims=True) + NORM_EPS) * g_ref[...]
    xb = xn.astype(BF16)
    q_ref[...] = (_dot(xb, wq_ref[...]) * Q_SCALE + qaug_ref[...]).astype(BF16)
    k = _dot(xb, wk_ref[...])
    k_ref[...] = (k + kaug_ref[...]).astype(BF16)
    kmean_ref[0] = jnp.mean(k, axis=0, keepdims=True)
    vt_ref[0] = (_dot_nt(wvt_ref[...], xb) + vaug_ref[...]).astype(BF16)
    pr = _dot(xb, wr_ref[...])

    @pl.when(i % tiles_per_seq == 0)
    def _():
        carry_ref[...] = jnp.zeros_like(carry_ref)

    tm = pr.shape[0]
    rolled = pltpu.roll(pr, 1, 0)
    row = lax.broadcasted_iota(jnp.int32, pr.shape, 0)
    prev = jnp.where(row == 0, carry_ref[...], rolled)
    carry_ref[...] = pr[tm - 1:tm, :]
    xs_ref[...] = pr + (prev - pr) * mu_ref[...]


def _in_projection(x2, norm_g, w_in, mu, batch, seq):
    T, D = x2.shape
    tm = MOBA_BLOCK
    nb = seq // tm
    GW = GROUP_WIDTH
    AW = ATTN_TILE_WIDTH

    def spread(w):
        w = w.reshape(D, N_HEADS, HEAD_DIM)
        return jnp.pad(w, ((0, 0), (0, 0), (0, HEAD_DIM))).reshape(D, AW).astype(BF16)

    wq = spread(w_in[:, :GW])
    wk = spread(w_in[:, GW:2 * GW])
    pos = jnp.arange(tm, dtype=F32)[:, None, None]
    slope = jnp.asarray(_alibi_slopes(), F32)[None, :, None]
    lane = jnp.arange(2 * HEAD_DIM)[None, None, :] - HEAD_DIM
    bias = slope * pos * LOG2E
    b_hi = lax.bitcast_convert_type(
        lax.bitcast_convert_type(bias, jnp.uint32) & jnp.uint32(0xFFFF0000), F32)
    b_lo = bias - b_hi
    pick = lambda l0, l1, l2, l3: jnp.where(lane == 0, l0, jnp.where(lane == 1, l1, jnp.where(
        lane == 2, l2, jnp.where(lane == 3, l3, 0.0)))).reshape(tm, AW)
    qaug = pick(-b_hi, -b_lo, 1.0, 1.0)
    kaug = pick(1.0, 1.0, b_hi, b_lo)
    wvt = jnp.pad(w_in[:, 2 * GW:3 * GW].T.reshape(N_HEADS, HEAD_DIM, D),
                  ((0, 0), (0, V_ROWS - HEAD_DIM), (0, 0))).reshape(N_HEADS * V_ROWS, D).astype(BF16)
    vaug = jnp.broadcast_to((jnp.arange(N_HEADS * V_ROWS) % V_ROWS == HEAD_DIM).astype(F32)[:, None],
                            (N_HEADS * V_ROWS, tm))
    rw = 3 * GW + RWKV_LORA_PAD
    n_r = w_in.shape[1] - 3 * GW
    wr = jnp.pad(w_in[:, 3 * GW:], ((0, 0), (0, rw - n_r))).astype(BF16)
    mu_p = jnp.pad(mu, (0, rw - n_r)).reshape(1, rw)
    const = lambda i: (0, 0)
    return pl.pallas_call(
        functools.partial(_inproj_kernel, tiles_per_seq=nb),
        grid=(T // tm,),
        in_specs=[
            pl.BlockSpec((tm, D), lambda i: (i, 0)),
            pl.BlockSpec((1, D), const),
            pl.BlockSpec((D, AW), const),
            pl.BlockSpec((D, AW), const),
            pl.BlockSpec((N_HEADS * V_ROWS, D), const),
            pl.BlockSpec((D, rw), const),
            pl.BlockSpec((1, rw), const),
            pl.BlockSpec((tm, AW), const),
            pl.BlockSpec((tm, AW), const),
            pl.BlockSpec((N_HEADS * V_ROWS, tm), const),
        ],
        out_specs=[
            pl.BlockSpec((tm, AW), lambda i: (i, 0)),
            pl.BlockSpec((tm, AW), lambda i: (i, 0)),
            pl.BlockSpec((1, N_HEADS * V_ROWS, tm), lambda i: (i, 0, 0)),
            pl.BlockSpec((1, 1, AW), lambda i: (i, 0, 0)),
            pl.BlockSpec((tm, rw), lambda i: (i, 0)),
        ],
        out_shape=[
            jax.ShapeDtypeStruct((T, AW), BF16),
            jax.ShapeDtypeStruct((T, AW), BF16),
            jax.ShapeDtypeStruct((T // tm, N_HEADS * V_ROWS, tm), BF16),
            jax.ShapeDtypeStruct((T // tm, 1, AW), F32),
            jax.ShapeDtypeStruct((T, rw), F32),
        ],
        scratch_shapes=[pltpu.VMEM((1, rw), F32)],
        compiler_params=_cparams("arbitrary"),
        name="in_projection",
    )(x2, norm_g.reshape(1, D), wq, wk, wvt, wr, mu_p, qaug, kaug, vaug)


def _moba_kernel(q_ref, k_ref, vt_ref, kmean_ref, g_ref,
                 o_ref, selb_ref, m_ref, acc_ref, ot_ref, *, nb, nbp, slopes):
    i = pl.program_id(1)
    blk = MOBA_BLOCK
    heads = range(N_HEADS)
    tile = [slice(h * 2 * HEAD_DIM, (h + 1) * 2 * HEAD_DIM) for h in heads]
    vrow = [slice(h * V_ROWS, (h + 1) * V_ROWS) for h in heads]
    n_iota = lax.broadcasted_iota(jnp.int32, (nbp, blk), 0)
    valid = n_iota < i

    gates = []
    for h in heads:
        km_hi, km_lo = _split2(kmean_ref[0, :, tile[h]])
        qh = q_ref[:, tile[h]]
        gates.append(jnp.where(valid, _dot_nt(km_hi, qh) + _dot_nt(km_lo, qh), NEG_INF))
    for h in heads:
        rank = jnp.zeros((nbp, blk), jnp.int32)
        for m in range(nb):
            gm = gates[h][m:m + 1, :]
            tie = (n_iota > m).astype(jnp.int32)
            rank = rank + jnp.where(gm > gates[h], 1, jnp.where(gm == gates[h], tie, 0))
        selb_ref[h] = jnp.where(rank < MOBA_TOPK, jnp.where(valid, 0.0, NEG_INF), NEG_INF)

    own = pl.ds(pl.multiple_of(i * blk, blk), blk)
    future = (lax.broadcasted_iota(jnp.int32, (blk, blk), 0) >
              lax.broadcasted_iota(jnp.int32, (blk, blk), 1))
    st = [jnp.where(future, NEG_INF, _dot_nt(k_ref[0, own, tile[h]], q_ref[:, tile[h]])) for h in heads]
    ps = []
    for h in heads:
        m0 = jnp.max(st[h], axis=0, keepdims=True)
        m_ref[h] = m0
        ps.append(jnp.exp2(st[h] - m0).astype(BF16))
    for h in heads:
        acc_ref[vrow[h], :] = _dot(vt_ref[0, i, vrow[h], :], ps[h])

    def body(j, _):
        rows = pl.ds(pl.multiple_of(j * blk, blk), blk)
        dist = (i - j).astype(F32) * (blk * LOG2E)
        st = [_dot_nt(k_ref[0, rows, tile[h]], q_ref[:, tile[h]]) for h in heads]
        ps, alphas = [], []
        for h in heads:
            off = selb_ref[h, pl.ds(j, 1), :] - slopes[h] * dist
            m_old = m_ref[h]
            m_new = jnp.maximum(m_old, jnp.max(st[h], axis=0, keepdims=True) + off)
            m_ref[h] = m_new
            ps.append(jnp.exp2(st[h] - (m_new - off)).astype(BF16))
            alphas.append(jnp.exp2(m_old - m_new))
        for h in heads:
            acc_ref[vrow[h], :] = alphas[h] * acc_ref[vrow[h], :] + _dot(vt_ref[0, j, vrow[h], :], ps[h])
        return 0

    lax.fori_loop(0, i, body, 0)

    for h in heads:
        v0 = h * V_ROWS
        o = acc_ref[v0:v0 + HEAD_DIM, :] / acc_ref[v0 + HEAD_DIM:v0 + HEAD_DIM + 1, :]
        ot_ref[h * HEAD_DIM:(h + 1) * HEAD_DIM, :] = o * lax.rsqrt(
            jnp.mean(o * o, axis=0, keepdims=True) + NORM_EPS)
    o_ref[...] = (ot_ref[...].T * g_ref[...]).astype(o_ref.dtype)


def _alibi_slopes():
    return [2.0 ** (-8.0 * (h + 1) / N_HEADS) for h in range(N_HEADS)]


def _moba_attention(q, k, vt, kmean, attn_norm_g, batch, seq):
    GW = GROUP_WIDTH
    AW = ATTN_TILE_WIDTH
    blk = MOBA_BLOCK
    nb = seq // blk
    nbp = -(-nb // 16) * 16
    k3 = k.reshape(batch, seq, AW)
    VW = N_HEADS * V_ROWS
    vt4 = vt.reshape(batch, nb, VW, blk)
    km = jnp.pad(kmean.reshape(batch, nb, AW), ((0, 0), (0, nbp - nb), (0, 0)))
    return pl.pallas_call(
        functools.partial(_moba_kernel, nb=nb, nbp=nbp, slopes=_alibi_slopes()),
        grid=(batch, nb),
        in_specs=[
            pl.BlockSpec((blk, AW), lambda b, i: (b * nb + i, 0)),
            pl.BlockSpec((1, seq, AW), lambda b, i: (b, 0, 0)),
            pl.BlockSpec((1, nb, VW, blk), lambda b, i: (b, 0, 0, 0)),
            pl.BlockSpec((1, nbp, AW), lambda b, i: (b, 0, 0)),
            pl.BlockSpec((1, GW), lambda b, i: (0, 0)),
        ],
        out_specs=pl.BlockSpec((blk, GW), lambda b, i: (b * nb + i, 0)),
        out_shape=jax.ShapeDtypeStruct((batch * seq, GW), BF16),
        scratch_shapes=[pltpu.VMEM((N_HEADS, nbp, blk), F32),
                        pltpu.VMEM((N_HEADS, 1, blk), F32),
                        pltpu.VMEM((VW, blk), F32),
                        pltpu.VMEM((GW, blk), F32)],
        compiler_params=_cparams("arbitrary", "arbitrary"),
        name="moba_attention",
    )(q, k3, vt4, km, attn_norm_g.reshape(1, GW))


def _mm(a, b, dims=((1,), (0,)), passes=3):
    dn = (dims, ((), ()))
    dg = lambda u, v: lax.dot_general(u, v, dn, preferred_element_type=F32)
    if passes == 1:
        return dg(a.astype(BF16), b.astype(BF16))
    a_hi, a_lo = _split2(a)
    b_hi, b_lo = _split2(b)
    return dg(a_hi, b_hi) + dg(a_hi, b_lo) + dg(a_lo, b_hi)


_NT = ((1,), (1,))
_TN = ((0,), (0,))
RWKV_PASSES = {"scores": 1, "inverse": 1, "outer": 1, "state": 1}


def _head_sum(x, pair_ones):
    x_hi, x_lo = _split2(x)
    w = pair_ones.shape[0]
    tiles = [_dot(x_hi[:, t:t + w], pair_ones) + _dot(x_lo[:, t:t + w], pair_ones)
             for t in range(0, x.shape[1], w)]
    return jnp.concatenate(tiles, axis=1)


def _rwkv_kernel(xs_ref, w2_ref, a2_ref, g2_ref, vec_ref, tril_ref, hsum_ref,
                 o_ref, state_ref, cum_ref, lw_ref, r_ref, kk_ref, b_ref, k2_ref, v_ref, oc_ref):
    GW = GROUP_WIDTH
    L = RWKV_CHUNK
    N = HEAD_DIM
    tm = xs_ref.shape[0]
    w0, a0, k_k, k_a, r_k, ln_g, ln_b = [vec_ref[n:n + 1, :] for n in range(7)]

    @pl.when(pl.program_id(1) == 0)
    def _():
        state_ref[...] = jnp.zeros_like(state_ref)

    r = xs_ref[:, 0:GW]
    k = xs_ref[:, GW:2 * GW]
    v = xs_ref[:, 2 * GW:3 * GW]
    lo = xs_ref[:, 3 * GW:]
    hsum = hsum_ref[...]
    logw = -RWKV_DECAY_SCALE * jax.nn.sigmoid(w0 + _dot(jnp.tanh(lo).astype(BF16), w2_ref[...]))
    a = jax.nn.sigmoid(a0 + _dot(lo.astype(BF16), a2_ref[...]))
    gate = _dot(jax.nn.sigmoid(lo).astype(BF16), g2_ref[...])
    kk = k * k_k
    kk = kk * lax.rsqrt(jnp.maximum(_head_sum(kk * kk, hsum), 1e-24))
    k2 = k * (1.0 + (a - 1.0) * k_a)
    bonus = _head_sum(r * k2 * r_k, hsum) * v
    lw_hi, lw_lo = _split2(logw)
    cum_ref[...] = _dot(tril_ref[...], lw_hi) + _dot(tril_ref[...], lw_lo)
    lw_ref[...] = logw
    r_ref[...] = r
    kk_ref[...] = kk
    b_ref[...] = kk * a
    k2_ref[...] = k2
    v_ref[...] = v

    row = lax.broadcasted_iota(jnp.int32, (L, L), 0)
    col = lax.broadcasted_iota(jnp.int32, (L, L), 1)
    strict = row > col
    incl = row >= col
    eye = row == col

    U = RWKV_CHUNKS_PER_STEP
    hsl = [slice(h * N, (h + 1) * N) for h in range(N_HEADS)]
    units = [(u, h) for u in range(U) for h in range(N_HEADS)]

    def step(c, _):
        pre = []
        for u in range(U):
            rows = pl.ds(pl.multiple_of((c * U + u) * L, L), L)
            cum = cum_ref[rows, :]
            cum_end = cum[L - 1:L, :]
            w_incl = jnp.exp(cum)
            w_excl = jnp.exp(cum - lw_ref[rows, :])
            w_inv = jnp.exp(-cum)
            e_end = jnp.exp(cum_end - cum)
            pre.append(dict(
                rows=rows, w_end=jnp.exp(cum_end),
                rhat=r_ref[rows, :] * w_incl, kkhat=kk_ref[rows, :] * w_excl,
                btil=b_ref[rows, :] * w_inv, ktil=k2_ref[rows, :] * w_inv,
                bbar=b_ref[rows, :] * e_end, kbar=k2_ref[rows, :] * e_end, v=v_ref[rows, :]))
        sl = lambda name, u, h: pre[u][name][:, hsl[h]]
        ma = [_mm(jnp.concatenate([sl("kkhat", u, h), sl("rhat", u, h)], axis=0),
                  jnp.concatenate([sl("btil", u, h), sl("ktil", u, h)], axis=0), _NT, RWKV_PASSES["scores"])
              for u, h in units]
        m_bk = [jnp.where(strict, m[:L, :L], 0.0) for m in ma]
        m_kk = [jnp.where(strict, m[:L, L:], 0.0) for m in ma]
        a_br = [jnp.where(incl, m[L:, :L], 0.0) for m in ma]
        a_kr = [jnp.where(incl, m[L:, L:], 0.0) for m in ma]
        n_u = range(len(units))
        vh = [sl("v", u, h) for u, h in units]
        pi = RWKV_PASSES["inverse"]
        y = [jnp.concatenate([sl("kkhat", *units[n]), _mm(m_kk[n], vh[n], passes=pi)], axis=1) for n in n_u]
        y = [y[n] - _mm(m_bk[n], y[n], passes=pi) for n in n_u]
        pw = m_bk
        for _ in range(int(math.log2(L)) - 1):
            pw = [_mm(pw[n], pw[n], passes=pi) for n in n_u]
            y = [y[n] + _mm(pw[n], y[n], passes=pi) for n in n_u]
        po = RWKV_PASSES["outer"]
        xtb = [_mm(y[n], sl("bbar", *units[n]), _TN, po) for n in n_u]
        vtk = [_mm(vh[n], sl("kbar", *units[n]), _TN, po) for n in n_u]
        ax = [_mm(a_br[n], y[n], passes=po) for n in n_u]
        akv = [_mm(a_kr[n], vh[n], passes=po) for n in n_u]
        ps = RWKV_PASSES["state"]
        state = [state_ref[h] for h in range(N_HEADS)]
        for n, (u, h) in enumerate(units):
            r2 = sl("rhat", u, h) - ax[n][:, :N]
            oc_ref[pre[u]["rows"], hsl[h]] = _mm(r2, state[h], _NT, ps) + akv[n] - ax[n][:, N:]
            p_mat = jnp.where(eye, pre[u]["w_end"][:, hsl[h]], 0.0) - xtb[n][:N]
            state[h] = _mm(state[h], p_mat, passes=ps) + vtk[n] - xtb[n][N:]
        for h in range(N_HEADS):
            state_ref[h] = state[h]
        return 0

    lax.fori_loop(0, tm // (L * U), step, 0)

    o = oc_ref[...]
    mean = _head_sum(o, hsum) * (1.0 / N)
    d = o - mean
    var = _head_sum(d * d, hsum) * (1.0 / N)
    o = d * lax.rsqrt(var + RWKV_GN_EPS) * ln_g + ln_b
    o_ref[...] = ((o + bonus) * gate).astype(o_ref.dtype)


def _rwkv_mix(xs, w0, w2, a0, a2, g2, k_k, k_a, r_k, ln_g, ln_b, batch, seq):
    T, rw = xs.shape
    GW = GROUP_WIDTH
    L = RWKV_CHUNK
    tm = 256
    nl = rw - 3 * GW
    d0, d1, d2 = RWKV_LORA
    w2p = jnp.pad(w2, ((0, nl - d0), (0, 0))).astype(BF16)
    a2p = jnp.pad(a2, ((d0, nl - d0 - d1), (0, 0))).astype(BF16)
    g2p = jnp.pad(g2, ((d0 + d1, nl - d0 - d1 - d2), (0, 0))).astype(BF16)
    vecs = jnp.stack([w0, a0, k_k, k_a, r_k.reshape(GW), ln_g, ln_b, jnp.zeros((GW,), F32)])
    t_idx = jnp.arange(tm)
    tril = ((t_idx[:, None] >= t_idx[None, :]) & (t_idx[:, None] // L == t_idx[None, :] // L)).astype(BF16)
    c_idx = jnp.arange(2 * HEAD_DIM) // HEAD_DIM
    hsum = (c_idx[:, None] == c_idx[None, :]).astype(BF16)
    tps = seq // tm
    const = lambda b, i: (0, 0)
    scr = lambda: pltpu.VMEM((tm, GW), F32)
    return pl.pallas_call(
        _rwkv_kernel,
        grid=(batch, tps),
        in_specs=[
            pl.BlockSpec((tm, rw), lambda b, i: (b * tps + i, 0)),
            pl.BlockSpec((nl, GW), const),
            pl.BlockSpec((nl, GW), const),
            pl.BlockSpec((nl, GW), const),
            pl.BlockSpec((8, GW), const),
            pl.BlockSpec((tm, tm), const),
            pl.BlockSpec((2 * HEAD_DIM, 2 * HEAD_DIM), const),
        ],
        out_specs=pl.BlockSpec((tm, GW), lambda b, i: (b * tps + i, 0)),
        out_shape=jax.ShapeDtypeStruct((T, GW), BF16),
        scratch_shapes=[pltpu.VMEM((N_HEADS, HEAD_DIM, HEAD_DIM), F32)] + [scr() for _ in range(8)],
        compiler_params=_cparams("arbitrary", "arbitrary"),
        name="rwkv7_mix",
    )(xs, w2p, a2p, g2p, vecs, tril, hsum)


EXPERT_LANES = 128

ROW_SUBLANES = 8
LANES = 128


def _store_row_tiles(ref, x):
    n = x.shape[0]
    for s in range(ROW_SUBLANES):
        ref[pl.ds(s, n, stride=ROW_SUBLANES), :] = x[:, s * LANES:(s + 1) * LANES]


def _load_row_tiles(ref, n):
    return jnp.concatenate([ref[pl.ds(s, n, stride=ROW_SUBLANES), :] for s in range(ROW_SUBLANES)], axis=1)


def _outproj_router_kernel(oa_ref, or_ref, x_ref, wa_ref, wr_ref, g_ref, wrt_ref, brt_ref,
                           h_ref, xn_ref, idx_ref, gate_ref, cnt_ref):
    h = x_ref[...] + _dot(oa_ref[...], wa_ref[...]) + _dot(or_ref[...], wr_ref[...])
    h_ref[...] = h
    xn = h * lax.rsqrt(jnp.mean(h * h, axis=-1, keepdims=True) + NORM_EPS) * g_ref[...]
    _store_row_tiles(xn_ref, xn)
    logits = _mm(xn, wrt_ref[...]) + brt_ref[...]
    tm = logits.shape[0]
    lane = lax.broadcasted_iota(jnp.int32, logits.shape, 1)
    lane4 = lax.broadcasted_iota(jnp.int32, (tm, TOP_K), 1)
    idx_out = jnp.zeros((tm, TOP_K), jnp.int32)
    val_out = jnp.zeros((tm, TOP_K), F32)
    member = jnp.zeros(logits.shape, F32)
    top = None
    denom = jnp.zeros((tm, 1), F32)
    for kk in range(TOP_K):
        mx = jnp.max(logits, axis=-1, keepdims=True)
        idx = jnp.min(jnp.where(logits == mx, lane, EXPERT_LANES), axis=-1, keepdims=True)
        hit = lane == idx
        member = jnp.where(hit, 1.0, member)
        logits = jnp.where(hit, -jnp.inf, logits)
        top = mx if top is None else top
        e = jnp.exp(mx - top)
        denom = denom + e
        idx_out = jnp.where(lane4 == kk, idx, idx_out)
        val_out = jnp.where(lane4 == kk, e, val_out)
    idx_ref[...] = idx_out
    gate_ref[...] = val_out / denom

    @pl.when(pl.program_id(0) == 0)
    def _():
        cnt_ref[...] = jnp.zeros_like(cnt_ref)

    cnt_ref[...] += jnp.sum(member, axis=0, keepdims=True)


def _outproj_router(o_att, o_rwkv, x2, w_out, norm_g, w_router, b_router):
    T, D = x2.shape
    GW = GROUP_WIDTH
    tm = 512
    E = w_router.shape[1]
    wa = w_out[:GW].astype(BF16)
    wr = w_out[GW:].astype(BF16)
    wrt = jnp.pad(w_router, ((0, 0), (0, EXPERT_LANES - E)))
    brt = jnp.pad(b_router, (0, EXPERT_LANES - E), constant_values=NEG_INF).reshape(1, EXPERT_LANES)
    const = lambda i: (0, 0)
    tile = lambda w: pl.BlockSpec((tm, w), lambda i: (i, 0))
    return pl.pallas_call(
        _outproj_router_kernel,
        grid=(T // tm,),
        in_specs=[tile(GW), tile(GW), tile(D),
                  pl.BlockSpec((GW, D), const), pl.BlockSpec((GW, D), const),
                  pl.BlockSpec((1, D), const), pl.BlockSpec((D, EXPERT_LANES), const),
                  pl.BlockSpec((1, EXPERT_LANES), const)],
        out_specs=[tile(D), pl.BlockSpec((tm * ROW_SUBLANES, LANES), lambda i: (i, 0)),
                   tile(TOP_K), tile(TOP_K), pl.BlockSpec((1, EXPERT_LANES), const)],
        out_shape=[jax.ShapeDtypeStruct((T, D), F32), jax.ShapeDtypeStruct((T * ROW_SUBLANES, LANES), F32),
                   jax.ShapeDtypeStruct((T, TOP_K), jnp.int32), jax.ShapeDtypeStruct((T, TOP_K), F32),
                   jax.ShapeDtypeStruct((1, EXPERT_LANES), F32)],
        compiler_params=_cparams("arbitrary"),
        name="outproj_router",
    )(o_att, o_rwkv, x2, wa, wr, norm_g.reshape(1, D), wrt, brt)


def _route_kernel(idx_ref, start_ref, tril_ref, dest_ref, carry_ref):
    @pl.when(pl.program_id(0) == 0)
    def _():
        carry_ref[...] = jnp.zeros_like(carry_ref)

    idx = idx_ref[...]
    tm = idx.shape[0]
    lane = lax.broadcasted_iota(jnp.int32, (tm, EXPERT_LANES), 1)
    lane4 = lax.broadcasted_iota(jnp.int32, (tm, TOP_K), 1)
    hits = [lane == idx[:, kk:kk + 1] for kk in range(TOP_K)]
    member = sum(jnp.where(hh, 1.0, 0.0) for hh in hits)
    before = _dot(tril_ref[...], member.astype(BF16)) + carry_ref[...]
    base = before + start_ref[...]
    dest = jnp.zeros((tm, TOP_K), jnp.int32)
    for kk in range(TOP_K):
        d = jnp.sum(jnp.where(hits[kk], base, 0.0), axis=-1, keepdims=True)
        dest = jnp.where(lane4 == kk, d.astype(jnp.int32), dest)
    dest_ref[...] = dest
    carry_ref[...] += jnp.sum(member, axis=0, keepdims=True)


def _route(idx, pad_start):
    T = idx.shape[0]
    tm = 512
    t_idx = jnp.arange(tm)
    tril = (t_idx[:, None] > t_idx[None, :]).astype(BF16)
    const = lambda i: (0, 0)
    return pl.pallas_call(
        _route_kernel,
        grid=(T // tm,),
        in_specs=[pl.BlockSpec((tm, TOP_K), lambda i: (i, 0)),
                  pl.BlockSpec((1, EXPERT_LANES), const),
                  pl.BlockSpec((tm, tm), const)],
        out_specs=pl.BlockSpec((tm, TOP_K), lambda i: (i, 0)),
        out_shape=jax.ShapeDtypeStruct((T, TOP_K), jnp.int32),
        scratch_shapes=[pltpu.VMEM((1, EXPERT_LANES), F32)],
        compiler_params=_cparams("arbitrary"),
        name="moe_route",
    )(idx, pad_start, tril)


DMA_ROWS_PER_STEP = 4
BLOCK_TILE_ROWS = MOE_BLOCK * ROW_SUBLANES


def _start_row_copies(make_copy, n_rows):
    def issue(g, _):
        for u in range(DMA_ROWS_PER_STEP):
            for kk in range(TOP_K):
                make_copy(g * DMA_ROWS_PER_STEP + u, kk).start(priority=kk % 2)
        return 0

    lax.fori_loop(0, n_rows // DMA_ROWS_PER_STEP, issue, 0)


def _wait_row_copies(make_copy, n_rows):
    def drain(g, _):
        for u in range(DMA_ROWS_PER_STEP):
            for kk in range(TOP_K):
                make_copy(g * DMA_ROWS_PER_STEP + u, kk).wait()
        return 0

    lax.fori_loop(0, n_rows // DMA_ROWS_PER_STEP, drain, 0)


def _row_tile(ref, r):
    return ref.at[pl.ds(pl.multiple_of(r * ROW_SUBLANES, ROW_SUBLANES), ROW_SUBLANES), :]


def _scatter_kernel(zero_ref, dest_ref, x_ref, slots_ref, zbuf_ref, sem, zsem):
    @pl.when(pl.program_id(0) == 0)
    def _():
        zbuf_ref[...] = jnp.zeros_like(zbuf_ref)

        def zero_copy(z):
            first = pl.multiple_of(zero_ref[z] * BLOCK_TILE_ROWS, BLOCK_TILE_ROWS)
            return pltpu.make_async_copy(zbuf_ref, slots_ref.at[pl.ds(first, BLOCK_TILE_ROWS), :], zsem)

        for z in range(zero_ref.shape[0]):
            @pl.when(zero_ref[z] >= 0)
            def _():
                zero_copy(z).start()
        for z in range(zero_ref.shape[0]):
            @pl.when(zero_ref[z] >= 0)
            def _():
                zero_copy(z).wait()

    def row_copy(t, kk):
        return pltpu.make_async_copy(_row_tile(x_ref, t), _row_tile(slots_ref, dest_ref[t * TOP_K + kk]), sem)

    n_rows = x_ref.shape[0] // ROW_SUBLANES
    _start_row_copies(row_copy, n_rows)
    _wait_row_copies(row_copy, n_rows)


def _scatter_rows(xn_tiles, dest_flat, zero_blocks, n_slots):
    T = xn_tiles.shape[0] // ROW_SUBLANES
    tm = 256
    grid_spec = pltpu.PrefetchScalarGridSpec(
        num_scalar_prefetch=1,
        grid=(T // tm,),
        in_specs=[pl.BlockSpec((tm * TOP_K,), lambda i, z: (i,), memory_space=pltpu.SMEM),
                  pl.BlockSpec((tm * ROW_SUBLANES, LANES), lambda i, z: (i, 0))],
        out_specs=pl.BlockSpec(memory_space=pl.ANY),
        scratch_shapes=[pltpu.VMEM((BLOCK_TILE_ROWS, LANES), F32),
                        pltpu.SemaphoreType.DMA(()), pltpu.SemaphoreType.DMA(())],
    )
    return pl.pallas_call(
        _scatter_kernel,
        grid_spec=grid_spec,
        out_shape=jax.ShapeDtypeStruct((n_slots * ROW_SUBLANES, LANES), F32),
        compiler_params=_cparams("arbitrary"),
        name="moe_scatter",
    )(zero_blocks, dest_flat, xn_tiles)


def _expert_kernel(be_ref, nu_ref, x_ref, wg_ref, bg_ref, wu_ref, bu_ref, wd_ref, bd_ref, y_ref,
                   wg_bf, wu_bf, wd_bf):
    i = pl.program_id(0)
    used = i < nu_ref[0]

    @pl.when(jnp.logical_not(used))
    def _():
        y_ref[...] = jnp.zeros_like(y_ref)

    @pl.when(jnp.logical_and(used, jnp.logical_or(i == 0, be_ref[i] != be_ref[jnp.maximum(i - 1, 0)])))
    def _():
        wg_bf[...] = wg_ref[0].astype(BF16)
        wu_bf[...] = wu_ref[0].astype(BF16)
        wd_bf[...] = wd_ref[0].astype(BF16)

    @pl.when(used)
    def _():
        xb = _load_row_tiles(x_ref, MOE_BLOCK).astype(BF16)
        gate = jnp.minimum(_dot(xb, wg_bf[...]) + bg_ref[0], SWIGLU_LIMIT)
        up = jnp.clip(_dot(xb, wu_bf[...]) + bu_ref[0], -SWIGLU_LIMIT, SWIGLU_LIMIT)
        glu = gate * jax.nn.sigmoid(SWIGLU_ALPHA * gate)
        _store_row_tiles(y_ref, _dot(((up + 1.0) * glu).astype(BF16), wd_bf[...]) + bd_ref[0])


def _expert_ffn(slots, block_expert, n_used, w_gate, b_gate, w_up, b_up, w_down, b_down):
    E, D, F = w_gate.shape
    n_blocks = slots.shape[0] // BLOCK_TILE_ROWS
    row = lambda i, be, nu: (jnp.minimum(i, nu[0] - 1), 0)
    wsel = lambda i, be, nu: (be[i], 0, 0)
    grid_spec = pltpu.PrefetchScalarGridSpec(
        num_scalar_prefetch=2,
        grid=(n_blocks,),
        in_specs=[pl.BlockSpec((BLOCK_TILE_ROWS, LANES), row),
                  pl.BlockSpec((1, D, F), wsel), pl.BlockSpec((1, 1, F), wsel),
                  pl.BlockSpec((1, D, F), wsel), pl.BlockSpec((1, 1, F), wsel),
                  pl.BlockSpec((1, F, D), wsel), pl.BlockSpec((1, 1, D), wsel)],
        out_specs=pl.BlockSpec((BLOCK_TILE_ROWS, LANES), lambda i, be, nu: (i, 0)),
        scratch_shapes=[pltpu.VMEM((D, F), BF16), pltpu.VMEM((D, F), BF16), pltpu.VMEM((F, D), BF16)],
    )
    return pl.pallas_call(
        _expert_kernel,
        grid_spec=grid_spec,
        out_shape=jax.ShapeDtypeStruct(slots.shape, F32),
        compiler_params=pltpu.CompilerParams(dimension_semantics=("arbitrary",),
                                             vmem_limit_bytes=EXPERT_VMEM_LIMIT),
        name="moe_experts",
    )(block_expert, n_used, slots, w_gate, b_gate.reshape(E, 1, F), w_up,
      b_up.reshape(E, 1, F), w_down, b_down.reshape(E, 1, D))


def _combine_kernel(dest_ref, dest_next_ref, h_ref, gate_ref, g_ref, y_ref, o_ref, buf_ref, sem):
    i = pl.program_id(0)
    tm = h_ref.shape[0]
    slot = i % 2

    def gather(dref, s):
        def row_copy(t, kk):
            return pltpu.make_async_copy(_row_tile(y_ref, dref[t * TOP_K + kk]),
                                         _row_tile(buf_ref.at[s, kk], t), sem.at[s])
        return row_copy

    @pl.when(i == 0)
    def _():
        _start_row_copies(gather(dest_ref, 0), tm)

    @pl.when(i + 1 < pl.num_programs(0))
    def _():
        _start_row_copies(gather(dest_next_ref, 1 - slot), tm)

    _wait_row_copies(gather(dest_ref, slot), tm)
    gates = gate_ref[...]
    h = h_ref[...]
    for kk in range(TOP_K):
        h = h + gates[:, kk:kk + 1] * _load_row_tiles(buf_ref.at[slot, kk], tm)
    o_ref[...] = h * lax.rsqrt(jnp.mean(h * h, axis=-1, keepdims=True) + NORM_EPS) * g_ref[...]


def _combine(h1, gates, dest_flat, y_slots, norm_g):
    T, D = h1.shape
    tm = 256
    n_tiles = T // tm
    return pl.pallas_call(
        _combine_kernel,
        grid=(n_tiles,),
        in_specs=[pl.BlockSpec((tm * TOP_K,), lambda i: (i,), memory_space=pltpu.SMEM),
                  pl.BlockSpec((tm * TOP_K,), lambda i: (jnp.minimum(i + 1, n_tiles - 1),),
                               memory_space=pltpu.SMEM),
                  pl.BlockSpec((tm, D), lambda i: (i, 0)),
                  pl.BlockSpec((tm, TOP_K), lambda i: (i, 0)),
                  pl.BlockSpec((1, D), lambda i: (0, 0)),
                  pl.BlockSpec(memory_space=pl.ANY)],
        out_specs=pl.BlockSpec((tm, D), lambda i: (i, 0)),
        out_shape=jax.ShapeDtypeStruct((T, D), F32),
        scratch_shapes=[pltpu.VMEM((2, TOP_K, tm * ROW_SUBLANES, LANES), F32),
                        pltpu.SemaphoreType.DMA((2,))],
        compiler_params=_cparams("arbitrary"),
        name="moe_combine",
    )(dest_flat, dest_flat, h1, gates, norm_g.reshape(1, D), y_slots)


def _moe(h1, xn2, idx, gates, counts, w_gate, b_gate, w_up, b_up, w_down, b_down, norm_final_g):
    T, D = h1.shape
    E = w_gate.shape[0]
    n_blocks = T * TOP_K // MOE_BLOCK + E
    cnt = counts[0].astype(jnp.int32)
    padded = (cnt + MOE_BLOCK - 1) // MOE_BLOCK * MOE_BLOCK
    pad_end = jnp.cumsum(padded)
    pad_start = (pad_end - padded).astype(F32).reshape(1, EXPERT_LANES)
    block_first = jnp.arange(n_blocks, dtype=jnp.int32) * MOE_BLOCK
    block_expert = jnp.minimum(
        jnp.sum((pad_end[None, :E] <= block_first[:, None]).astype(jnp.int32), axis=1), E - 1)
    n_used = (pad_end[E - 1] // MOE_BLOCK).astype(jnp.int32).reshape(1)
    last_block = jnp.where(padded[:E] > 0, pad_end[:E] // MOE_BLOCK - 1, -1)
    spare = n_used[0] + jnp.arange(E, dtype=jnp.int32)
    zero_blocks = jnp.concatenate([last_block, jnp.where(spare < n_blocks, spare, -1)]).astype(jnp.int32)
    dest = _route(idx, pad_start).reshape(T * TOP_K)
    slots = _scatter_rows(xn2, dest, zero_blocks, n_blocks * MOE_BLOCK)
    y_slots = _expert_ffn(slots, block_expert, n_used, w_gate, b_gate, w_up, b_up, w_down, b_down)
    return _combine(h1, gates, dest, y_slots, norm_final_g)


def kernel(x, norm_mix_g, w_in, attn_norm_g, rwkv_mu, rwkv_w0, rwkv_w2, rwkv_a0, rwkv_a2,
           rwkv_g2, rwkv_k_k, rwkv_k_a, rwkv_r_k, rwkv_ln_g, rwkv_ln_b, w_out, norm_ffn_g,
           w_router, b_router, moe_w_gate, moe_b_gate, moe_w_up, moe_b_up, moe_w_down,
           moe_b_down, norm_final_g):
    B, S, D = x.shape
    x2 = x.reshape(B * S, D)
    q, k, vt, km, xs = _in_projection(x2, norm_mix_g[0], w_in[0], rwkv_mu[0], B, S)
    o_att = _moba_attention(q, k, vt, km, attn_norm_g[0], B, S)
    o_rwkv = _rwkv_mix(xs, rwkv_w0[0], rwkv_w2[0], rwkv_a0[0], rwkv_a2[0], rwkv_g2[0], rwkv_k_k[0],
                       rwkv_k_a[0], rwkv_r_k[0], rwkv_ln_g[0], rwkv_ln_b[0], B, S)
    h1, xn2, idx, gates, counts = _outproj_router(o_att, o_rwkv, x2, w_out[0], norm_ffn_g[0],
                                                  w_router[0], b_router[0])
    out = _moe(h1, xn2, idx, gates, counts, moe_w_gate[0], moe_b_gate[0], moe_w_up[0], moe_b_up[0],
               moe_w_down[0], moe_b_down[0], norm_final_g)
    return out.reshape(B, S, D)
```

```python
import functools
import math

import jax
import jax.numpy as jnp
from jax import lax
from jax.experimental import pallas as pl
from jax.experimental.pallas import tpu as pltpu

F32 = jnp.float32
BF16 = jnp.bfloat16

HEAD_DIM = 64
N_HEADS = 8
GROUP_WIDTH = N_HEADS * HEAD_DIM
ATTN_TILE_WIDTH = N_HEADS * 2 * HEAD_DIM
V_ROWS = HEAD_DIM + 16
LOG2E = math.log2(math.e)
Q_SCALE = HEAD_DIM ** -0.5 * LOG2E
MOBA_BLOCK = 256
MOBA_TOPK = 3
N_EXPERTS = 32
TOP_K = 4
MOE_BLOCK = 512
SWIGLU_LIMIT = 7.0
SWIGLU_ALPHA = 1.702
NORM_EPS = 1e-6
NEG_INF = -1e30
RWKV_DECAY_SCALE = math.exp(-0.5)
RWKV_GN_EPS = HEAD_DIM * 1e-5
RWKV_LORA = (32, 32, 96)
RWKV_LORA_PAD = 256
RWKV_CHUNK = 64
RWKV_CHUNKS_PER_STEP = 4

VMEM_LIMIT = 48 * 1024 * 1024
EXPERT_VMEM_LIMIT = 56 * 1024 * 1024


def _cparams(*sem):
    return pltpu.CompilerParams(dimension_semantics=sem, vmem_limit_bytes=VMEM_LIMIT)


def _dot(a, b):
    return jnp.dot(a, b, preferred_element_type=F32)


def _dot_nt(a, b):
    return lax.dot_general(a, b, (((1,), (1,)), ((), ())), preferred_element_type=F32)


def _dot_tn(a, b):
    return lax.dot_general(a, b, (((0,), (0,)), ((), ())), preferred_element_type=F32)


def _split2(x):
    hi = x.astype(BF16)
    lo = (x - hi.astype(F32)).astype(BF16)
    return hi, lo


def _one_head_per_tile(x):
    pair = 2 * HEAD_DIM
    low = lax.broadcasted_iota(jnp.int32, (x.shape[0], pair), 1) < HEAD_DIM
    tiles = []
    for h in range(N_HEADS):
        src = x[:, (h // 2) * pair:(h // 2 + 1) * pair]
        if h % 2:
            src = pltpu.roll(src, HEAD_DIM, 1)
        tiles.append(jnp.where(low, src, 0.0))
    return jnp.concatenate(tiles, axis=1)


def _inproj_kernel(x_ref, g_ref, wq_ref, wk_ref, wvt_ref, wr_ref, mu_ref, qaug_ref, kaug_ref, vaug_ref,
                   q_ref, k_ref, vt_ref, kmean_ref, xs_ref, carry_ref, *, tiles_per_seq):
    i = pl.program_id(0)
    x = x_ref[...]
    xn = x * lax.rsqrt(jnp.mean(x * x, axis=-1, keepdims=True) + NORM_EPS) * g_ref[...]
    xb = xn.astype(BF16)
    q_ref[...] = (_one_head_per_tile(_dot(xb, wq_ref[...])) * Q_SCALE + qaug_ref[...]).astype(BF16)
    k = _one_head_per_tile(_dot(xb, wk_ref[...]))
    k_ref[...] = (k + kaug_ref[...]).astype(BF16)
    kmean_ref[0] = jnp.mean(k, axis=0, keepdims=True)
    vt_ref[0] = (_dot_nt(wvt_ref[...], xb) + vaug_ref[...]).astype(BF16)
    pr = _dot(xb, wr_ref[...])

    @pl.when(i % tiles_per_seq == 0)
    def _():
        carry_ref[...] = jnp.zeros_like(carry_ref)

    tm = pr.shape[0]
    rolled = pltpu.roll(pr, 1, 0)
    row = lax.broadcasted_iota(jnp.int32, pr.shape, 0)
    prev = jnp.where(row == 0, carry_ref[...], rolled)
    carry_ref[...] = pr[tm - 1:tm, :]
    xs_ref[...] = pr + (prev - pr) * mu_ref[...]


def _in_projection(x2, norm_g, w_in, mu, batch, seq):
    T, D = x2.shape
    tm = MOBA_BLOCK
    nb = seq // tm
    GW = GROUP_WIDTH
    AW = ATTN_TILE_WIDTH

    wq = w_in[:, :GW].astype(BF16)
    wk = w_in[:, GW:2 * GW].astype(BF16)
    pos = jnp.arange(tm, dtype=F32)[:, None, None]
    slope = jnp.asarray(_alibi_slopes(), F32)[None, :, None]
    lane = jnp.arange(2 * HEAD_DIM)[None, None, :] - HEAD_DIM
    bias = slope * pos * LOG2E
    b_hi = lax.bitcast_convert_type(
        lax.bitcast_convert_type(bias, jnp.uint32) & jnp.uint32(0xFFFF0000), F32)
    b_lo = bias - b_hi
    pick = lambda l0, l1, l2, l3: jnp.where(lane == 0, l0, jnp.where(lane == 1, l1, jnp.where(
        lane == 2, l2, jnp.where(lane == 3, l3, 0.0)))).reshape(tm, AW)
    qaug = pick(-b_hi, -b_lo, 1.0, 1.0)
    kaug = pick(1.0, 1.0, b_hi, b_lo)
    wvt = jnp.pad(w_in[:, 2 * GW:3 * GW].T.reshape(N_HEADS, HEAD_DIM, D),
                  ((0, 0), (0, V_ROWS - HEAD_DIM), (0, 0))).reshape(N_HEADS * V_ROWS, D).astype(BF16)
    vaug = jnp.broadcast_to((jnp.arange(N_HEADS * V_ROWS) % V_ROWS == HEAD_DIM).astype(F32)[:, None],
                            (N_HEADS * V_ROWS, tm))
    rw = 3 * GW + RWKV_LORA_PAD
    n_r = w_in.shape[1] - 3 * GW
    wr = jnp.pad(w_in[:, 3 * GW:], ((0, 0), (0, rw - n_r))).astype(BF16)
    mu_p = jnp.pad(mu, (0, rw - n_r)).reshape(1, rw)
    const = lambda i: (0, 0)
    return pl.pallas_call(
        functools.partial(_inproj_kernel, tiles_per_seq=nb),
        grid=(T // tm,),
        in_specs=[
            pl.BlockSpec((tm, D), lambda i: (i, 0)),
            pl.BlockSpec((1, D), const),
            pl.BlockSpec((D, GW), const),
            pl.BlockSpec((D, GW), const),
            pl.BlockSpec((N_HEADS * V_ROWS, D), const),
            pl.BlockSpec((D, rw), const),
            pl.BlockSpec((1, rw), const),
            pl.BlockSpec((tm, AW), const),
            pl.BlockSpec((tm, AW), const),
            pl.BlockSpec((N_HEADS * V_ROWS, tm), const),
        ],
        out_specs=[
            pl.BlockSpec((tm, AW), lambda i: (i, 0)),
            pl.BlockSpec((tm, AW), lambda i: (i, 0)),
            pl.BlockSpec((1, N_HEADS * V_ROWS, tm), lambda i: (i, 0, 0)),
            pl.BlockSpec((1, 1, AW), lambda i: (i, 0, 0)),
            pl.BlockSpec((tm, rw), lambda i: (i, 0)),
        ],
        out_shape=[
            jax.ShapeDtypeStruct((T, AW), BF16),
            jax.ShapeDtypeStruct((T, AW), BF16),
            jax.ShapeDtypeStruct((T // tm, N_HEADS * V_ROWS, tm), BF16),
            jax.ShapeDtypeStruct((T // tm, 1, AW), F32),
            jax.ShapeDtypeStruct((T, rw), F32),
        ],
        scratch_shapes=[pltpu.VMEM((1, rw), F32)],
        compiler_params=_cparams("arbitrary"),
        name="in_projection",
    )(x2, norm_g.reshape(1, D), wq, wk, wvt, wr, mu_p, qaug, kaug, vaug)


def _moba_kernel(q_ref, k_ref, vt_ref, kmean_ref, g_ref,
                 o_ref, selb_ref, m_ref, acc_ref, ot_ref, *, nb, nbp, slopes):
    i = pl.program_id(1)
    blk = MOBA_BLOCK
    heads = range(N_HEADS)
    tile = [slice(h * 2 * HEAD_DIM, (h + 1) * 2 * HEAD_DIM) for h in heads]
    vrow = [slice(h * V_ROWS, (h + 1) * V_ROWS) for h in heads]
    n_iota = lax.broadcasted_iota(jnp.int32, (nbp, blk), 0)
    valid = n_iota < i

    gates = []
    for h in heads:
        km_hi, km_lo = _split2(kmean_ref[0, :, tile[h]])
        qh = q_ref[:, tile[h]]
        gates.append(jnp.where(valid, _dot_nt(km_hi, qh) + _dot_nt(km_lo, qh), NEG_INF))
    for h in heads:
        rank = jnp.zeros((nbp, blk), jnp.int32)
        for m in range(nb):
            gm = gates[h][m:m + 1, :]
            tie = (n_iota > m).astype(jnp.int32)
            rank = rank + jnp.where(gm > gates[h], 1, jnp.where(gm == gates[h], tie, 0))
        selb_ref[h] = jnp.where(rank < MOBA_TOPK, jnp.where(valid, 0.0, NEG_INF), NEG_INF)

    own = pl.ds(pl.multiple_of(i * blk, blk), blk)
    future = (lax.broadcasted_iota(jnp.int32, (blk, blk), 0) >
              lax.broadcasted_iota(jnp.int32, (blk, blk), 1))
    st = [jnp.where(future, NEG_INF, _dot_nt(k_ref[0, own, tile[h]], q_ref[:, tile[h]])) for h in heads]
    ps = []
    for h in heads:
        m0 = jnp.max(st[h], axis=0, keepdims=True)
        m_ref[h] = m0
        ps.append(jnp.exp2(st[h] - m0).astype(BF16))
    for h in heads:
        acc_ref[vrow[h], :] = _dot(vt_ref[0, i, vrow[h], :], ps[h])

    def body(j, _):
        rows = pl.ds(pl.multiple_of(j * blk, blk), blk)
        dist = (i - j).astype(F32) * (blk * LOG2E)
        st = [_dot_nt(k_ref[0, rows, tile[h]], q_ref[:, tile[h]]) for h in heads]
        ps, alphas = [], []
        for h in heads:
            off = selb_ref[h, pl.ds(j, 1), :] - slopes[h] * dist
            m_old = m_ref[h]
            m_new = jnp.maximum(m_old, jnp.max(st[h], axis=0, keepdims=True) + off)
            m_ref[h] = m_new
            ps.append(jnp.exp2(st[h] - (m_new - off)).astype(BF16))
            alphas.append(jnp.exp2(m_old - m_new))
        for h in heads:
            acc_ref[vrow[h], :] = alphas[h] * acc_ref[vrow[h], :] + _dot(vt_ref[0, j, vrow[h], :], ps[h])
        return 0

    lax.fori_loop(0, i, body, 0)

    for h in heads:
        v0 = h * V_ROWS
        o = acc_ref[v0:v0 + HEAD_DIM, :] / acc_ref[v0 + HEAD_DIM:v0 + HEAD_DIM + 1, :]
        ot_ref[h * HEAD_DIM:(h + 1) * HEAD_DIM, :] = o * lax.rsqrt(
            jnp.mean(o * o, axis=0, keepdims=True) + NORM_EPS)
    o_ref[...] = (ot_ref[...].T * g_ref[...]).astype(o_ref.dtype)


def _alibi_slopes():
    return [2.0 ** (-8.0 * (h + 1) / N_HEADS) for h in range(N_HEADS)]


def _moba_attention(q, k, vt, kmean, attn_norm_g, batch, seq):
    GW = GROUP_WIDTH
    AW = ATTN_TILE_WIDTH
    blk = MOBA_BLOCK
    nb = seq // blk
    nbp = -(-nb // 16) * 16
    k3 = k.reshape(batch, seq, AW)
    VW = N_HEADS * V_ROWS
    vt4 = vt.reshape(batch, nb, VW, blk)
    km = jnp.pad(kmean.reshape(batch, nb, AW), ((0, 0), (0, nbp - nb), (0, 0)))
    return pl.pallas_call(
        functools.partial(_moba_kernel, nb=nb, nbp=nbp, slopes=_alibi_slopes()),
        grid=(batch, nb),
        in_specs=[
            pl.BlockSpec((blk, AW), lambda b, i: (b * nb + i, 0)),
            pl.BlockSpec((1, seq, AW), lambda b, i: (b, 0, 0)),
            pl.BlockSpec((1, nb, VW, blk), lambda b, i: (b, 0, 0, 0)),
            pl.BlockSpec((1, nbp, AW), lambda b, i: (b, 0, 0)),
            pl.BlockSpec((1, GW), lambda b, i: (0, 0)),
        ],
        out_specs=pl.BlockSpec((blk, GW), lambda b, i: (b * nb + i, 0)),
        out_shape=jax.ShapeDtypeStruct((batch * seq, GW), BF16),
        scratch_shapes=[pltpu.VMEM((N_HEADS, nbp, blk), F32),
                        pltpu.VMEM((N_HEADS, 1, blk), F32),
                        pltpu.VMEM((VW, blk), F32),
                        pltpu.VMEM((GW, blk), F32)],
        compiler_params=_cparams("arbitrary", "arbitrary"),
        name="moba_attention",
    )(q, k3, vt4, km, attn_norm_g.reshape(1, GW))


def _mm(a, b, dims=((1,), (0,)), passes=3):
    dn = (dims, ((), ()))
    dg = lambda u, v: lax.dot_general(u, v, dn, preferred_element_type=F32)
    if passes == 1:
        return dg(a.astype(BF16), b.astype(BF16))
    a_hi, a_lo = _split2(a)
    b_hi, b_lo = _split2(b)
    return dg(a_hi, b_hi) + dg(a_hi, b_lo) + dg(a_lo, b_hi)


_NT = ((1,), (1,))
_TN = ((0,), (0,))
RWKV_PASSES = {"scores": 1, "inverse": 1, "outer": 1, "state": 1}


def _head_sum(x, pair_ones):
    x_hi, x_lo = _split2(x)
    w = pair_ones.shape[0]
    tiles = [_dot(x_hi[:, t:t + w], pair_ones) + _dot(x_lo[:, t:t + w], pair_ones)
             for t in range(0, x.shape[1], w)]
    return jnp.concatenate(tiles, axis=1)


def _rwkv_kernel(xs_ref, w2_ref, a2_ref, g2_ref, vec_ref, tril_ref, hsum_ref,
                 o_ref, state_ref, cum_ref, lw_ref, r_ref, kk_ref, b_ref, k2_ref, v_ref, oc_ref):
    GW = GROUP_WIDTH
    L = RWKV_CHUNK
    N = HEAD_DIM
    tm = xs_ref.shape[0]
    w0, a0, k_k, k_a, r_k, ln_g, ln_b = [vec_ref[n:n + 1, :] for n in range(7)]

    @pl.when(pl.program_id(1) == 0)
    def _():
        state_ref[...] = jnp.zeros_like(state_ref)

    r = xs_ref[:, 0:GW]
    k = xs_ref[:, GW:2 * GW]
    v = xs_ref[:, 2 * GW:3 * GW]
    lo = xs_ref[:, 3 * GW:]
    hsum = hsum_ref[...]
    logw = -RWKV_DECAY_SCALE * jax.nn.sigmoid(w0 + _dot(jnp.tanh(lo).astype(BF16), w2_ref[...]))
    a = jax.nn.sigmoid(a0 + _dot(lo.astype(BF16), a2_ref[...]))
    gate = _dot(jax.nn.sigmoid(lo).astype(BF16), g2_ref[...])
    kk = k * k_k
    kk = kk * lax.rsqrt(jnp.maximum(_head_sum(kk * kk, hsum), 1e-24))
    k2 = k * (1.0 + (a - 1.0) * k_a)
    bonus = _head_sum(r * k2 * r_k, hsum) * v
    lw_hi, lw_lo = _split2(logw)
    cum_ref[...] = _dot(tril_ref[...], lw_hi) + _dot(tril_ref[...], lw_lo)
    lw_ref[...] = logw
    r_ref[...] = r
    kk_ref[...] = kk
    b_ref[...] = kk * a
    k2_ref[...] = k2
    v_ref[...] = v

    row = lax.broadcasted_iota(jnp.int32, (L, L), 0)
    col = lax.broadcasted_iota(jnp.int32, (L, L), 1)
    strict = row > col
    incl = row >= col
    eye = row == col

    U = RWKV_CHUNKS_PER_STEP
    hsl = [slice(h * N, (h + 1) * N) for h in range(N_HEADS)]
    units = [(u, h) for u in range(U) for h in range(N_HEADS)]

    def step(c, _):
        pre = []
        for u in range(U):
            rows = pl.ds(pl.multiple_of((c * U + u) * L, L), L)
            cum = cum_ref[rows, :]
            cum_end = cum[L - 1:L, :]
            w_incl = jnp.exp(cum)
            w_excl = jnp.exp(cum - lw_ref[rows, :])
            w_inv = jnp.exp(-cum)
            e_end = jnp.exp(cum_end - cum)
            pre.append(dict(
                rows=rows, w_end=jnp.exp(cum_end),
                rhat=r_ref[rows, :] * w_incl, kkhat=kk_ref[rows, :] * w_excl,
                btil=b_ref[rows, :] * w_inv, ktil=k2_ref[rows, :] * w_inv,
                bbar=b_ref[rows, :] * e_end, kbar=k2_ref[rows, :] * e_end, v=v_ref[rows, :]))
        sl = lambda name, u, h: pre[u][name][:, hsl[h]]
        ma = [_mm(jnp.concatenate([sl("kkhat", u, h), sl("rhat", u, h)], axis=0),
                  jnp.concatenate([sl("btil", u, h), sl("ktil", u, h)], axis=0), _NT, RWKV_PASSES["scores"])
              for u, h in units]
        m_bk = [jnp.where(strict, m[:L, :L], 0.0) for m in ma]
        m_kk = [jnp.where(strict, m[:L, L:], 0.0) for m in ma]
        a_br = [jnp.where(incl, m[L:, :L], 0.0) for m in ma]
        a_kr = [jnp.where(incl, m[L:, L:], 0.0) for m in ma]
        n_u = range(len(units))
        vh = [sl("v", u, h) for u, h in units]
        pi = RWKV_PASSES["inverse"]
        y = [jnp.concatenate([sl("kkhat", *units[n]), _mm(m_kk[n], vh[n], passes=pi)], axis=1) for n in n_u]
        y = [y[n] - _mm(m_bk[n], y[n], passes=pi) for n in n_u]
        pw = m_bk
        for _ in range(int(math.log2(L)) - 1):
            pw = [_mm(pw[n], pw[n], passes=pi) for n in n_u]
            y = [y[n] + _mm(pw[n], y[n], passes=pi) for n in n_u]
        po = RWKV_PASSES["outer"]
        xtb = [_mm(y[n], sl("bbar", *units[n]), _TN, po) for n in n_u]
        vtk = [_mm(vh[n], sl("kbar", *units[n]), _TN, po) for n in n_u]
        ax = [_mm(a_br[n], y[n], passes=po) for n in n_u]
        akv = [_mm(a_kr[n], vh[n], passes=po) for n in n_u]
        ps = RWKV_PASSES["state"]
        state = [state_ref[h] for h in range(N_HEADS)]
        for n, (u, h) in enumerate(units):
            r2 = sl("rhat", u, h) - ax[n][:, :N]
            oc_ref[pre[u]["rows"], hsl[h]] = _mm(r2, state[h], _NT, ps) + akv[n] - ax[n][:, N:]
            p_mat = jnp.where(eye, pre[u]["w_end"][:, hsl[h]], 0.0) - xtb[n][:N]
            state[h] = _mm(state[h], p_mat, passes=ps) + vtk[n] - xtb[n][N:]
        for h in range(N_HEADS):
            state_ref[h] = state[h]
        return 0

    lax.fori_loop(0, tm // (L * U), step, 0)

    o = oc_ref[...]
    mean = _head_sum(o, hsum) * (1.0 / N)
    d = o - mean
    var = _head_sum(d * d, hsum) * (1.0 / N)
    o = d * lax.rsqrt(var + RWKV_GN_EPS) * ln_g + ln_b
    o_ref[...] = ((o + bonus) * gate).astype(o_ref.dtype)


def _rwkv_mix(xs, w0, w2, a0, a2, g2, k_k, k_a, r_k, ln_g, ln_b, batch, seq):
    T, rw = xs.shape
    GW = GROUP_WIDTH
    L = RWKV_CHUNK
    tm = 256
    nl = rw - 3 * GW
    d0, d1, d2 = RWKV_LORA
    w2p = jnp.pad(w2, ((0, nl - d0), (0, 0))).astype(BF16)
    a2p = jnp.pad(a2, ((d0, nl - d0 - d1), (0, 0))).astype(BF16)
    g2p = jnp.pad(g2, ((d0 + d1, nl - d0 - d1 - d2), (0, 0))).astype(BF16)
    vecs = jnp.stack([w0, a0, k_k, k_a, r_k.reshape(GW), ln_g, ln_b, jnp.zeros((GW,), F32)])
    t_idx = jnp.arange(tm)
    tril = ((t_idx[:, None] >= t_idx[None, :]) & (t_idx[:, None] // L == t_idx[None, :] // L)).astype(BF16)
    c_idx = jnp.arange(2 * HEAD_DIM) // HEAD_DIM
    hsum = (c_idx[:, None] == c_idx[None, :]).astype(BF16)
    tps = seq // tm
    const = lambda b, i: (0, 0)
    scr = lambda: pltpu.VMEM((tm, GW), F32)
    return pl.pallas_call(
        _rwkv_kernel,
        grid=(batch, tps),
        in_specs=[
            pl.BlockSpec((tm, rw), lambda b, i: (b * tps + i, 0)),
            pl.BlockSpec((nl, GW), const),
            pl.BlockSpec((nl, GW), const),
            pl.BlockSpec((nl, GW), const),
            pl.BlockSpec((8, GW), const),
            pl.BlockSpec((tm, tm), const),
            pl.BlockSpec((2 * HEAD_DIM, 2 * HEAD_DIM), const),
        ],
        out_specs=pl.BlockSpec((tm, GW), lambda b, i: (b * tps + i, 0)),
        out_shape=jax.ShapeDtypeStruct((T, GW), BF16),
        scratch_shapes=[pltpu.VMEM((N_HEADS, HEAD_DIM, HEAD_DIM), F32)] + [scr() for _ in range(8)],
        compiler_params=_cparams("arbitrary", "arbitrary"),
        name="rwkv7_mix",
    )(xs, w2p, a2p, g2p, vecs, tril, hsum)


EXPERT_LANES = 128

ROW_SUBLANES = 8
LANES = 128


def _store_row_tiles(ref, x):
    n = x.shape[0]
    for s in range(ROW_SUBLANES):
        ref[pl.ds(s, n, stride=ROW_SUBLANES), :] = x[:, s * LANES:(s + 1) * LANES]


def _load_row_tiles(ref, n):
    return jnp.concatenate([ref[pl.ds(s, n, stride=ROW_SUBLANES), :] for s in range(ROW_SUBLANES)], axis=1)


def _outproj_router_kernel(oa_ref, or_ref, x_ref, wa_ref, wr_ref, g_ref, wrt_ref, brt_ref,
                           h_ref, xn_ref, idx_ref, gate_ref, cnt_ref):
    h = x_ref[...] + _dot(oa_ref[...], wa_ref[...]) + _dot(or_ref[...], wr_ref[...])
    h_ref[...] = h
    xn = h * lax.rsqrt(jnp.mean(h * h, axis=-1, keepdims=True) + NORM_EPS) * g_ref[...]
    _store_row_tiles(xn_ref, xn)
    logits = _mm(xn, wrt_ref[...]) + brt_ref[...]
    tm = logits.shape[0]
    lane = lax.broadcasted_iota(jnp.int32, logits.shape, 1)
    lane4 = lax.broadcasted_iota(jnp.int32, (tm, TOP_K), 1)
    idx_out = jnp.zeros((tm, TOP_K), jnp.int32)
    val_out = jnp.zeros((tm, TOP_K), F32)
    member = jnp.zeros(logits.shape, F32)
    top = None
    denom = jnp.zeros((tm, 1), F32)
    for kk in range(TOP_K):
        mx = jnp.max(logits, axis=-1, keepdims=True)
        idx = jnp.min(jnp.where(logits == mx, lane, EXPERT_LANES), axis=-1, keepdims=True)
        hit = lane == idx
        member = jnp.where(hit, 1.0, member)
        logits = jnp.where(hit, -jnp.inf, logits)
        top = mx if top is None else top
        e = jnp.exp(mx - top)
        denom = denom + e
        idx_out = jnp.where(lane4 == kk, idx, idx_out)
        val_out = jnp.where(lane4 == kk, e, val_out)
    idx_ref[...] = idx_out
    gate_ref[...] = val_out / denom

    @pl.when(pl.program_id(0) == 0)
    def _():
        cnt_ref[...] = jnp.zeros_like(cnt_ref)

    cnt_ref[...] += jnp.sum(member, axis=0, keepdims=True)


def _outproj_router(o_att, o_rwkv, x2, w_out, norm_g, w_router, b_router):
    T, D = x2.shape
    GW = GROUP_WIDTH
    tm = 512
    E = w_router.shape[1]
    wa = w_out[:GW].astype(BF16)
    wr = w_out[GW:].astype(BF16)
    wrt = jnp.pad(w_router, ((0, 0), (0, EXPERT_LANES - E)))
    brt = jnp.pad(b_router, (0, EXPERT_LANES - E), constant_values=NEG_INF).reshape(1, EXPERT_LANES)
    const = lambda i: (0, 0)
    tile = lambda w: pl.BlockSpec((tm, w), lambda i: (i, 0))
    return pl.pallas_call(
        _outproj_router_kernel,
        grid=(T // tm,),
        in_specs=[tile(GW), tile(GW), tile(D),
                  pl.BlockSpec((GW, D), const), pl.BlockSpec((GW, D), const),
                  pl.BlockSpec((1, D), const), pl.BlockSpec((D, EXPERT_LANES), const),
                  pl.BlockSpec((1, EXPERT_LANES), const)],
        out_specs=[tile(D), pl.BlockSpec((tm * ROW_SUBLANES, LANES), lambda i: (i, 0)),
                   tile(TOP_K), tile(TOP_K), pl.BlockSpec((1, EXPERT_LANES), const)],
        out_shape=[jax.ShapeDtypeStruct((T, D), F32), jax.ShapeDtypeStruct((T * ROW_SUBLANES, LANES), F32),
                   jax.ShapeDtypeStruct((T, TOP_K), jnp.int32), jax.ShapeDtypeStruct((T, TOP_K), F32),
                   jax.ShapeDtypeStruct((1, EXPERT_LANES), F32)],
        compiler_params=_cparams("arbitrary"),
        name="outproj_router",
    )(o_att, o_rwkv, x2, wa, wr, norm_g.reshape(1, D), wrt, brt)


def _route_kernel(idx_ref, start_ref, tril_ref, dest_ref, carry_ref):
    @pl.when(pl.program_id(0) == 0)
    def _():
        carry_ref[...] = jnp.zeros_like(carry_ref)

    idx = idx_ref[...]
    tm = idx.shape[0]
    lane = lax.broadcasted_iota(jnp.int32, (tm, EXPERT_LANES), 1)
    lane4 = lax.broadcasted_iota(jnp.int32, (tm, TOP_K), 1)
    hits = [lane == idx[:, kk:kk + 1] for kk in range(TOP_K)]
    member = sum(jnp.where(hh, 1.0, 0.0) for hh in hits)
    before = _dot(tril_ref[...], member.astype(BF16)) + carry_ref[...]
    base = before + start_ref[...]
    dest = jnp.zeros((tm, TOP_K), jnp.int32)
    for kk in range(TOP_K):
        d = jnp.sum(jnp.where(hits[kk], base, 0.0), axis=-1, keepdims=True)
        dest = jnp.where(lane4 == kk, d.astype(jnp.int32), dest)
    dest_ref[...] = dest
    carry_ref[...] += jnp.sum(member, axis=0, keepdims=True)


def _route(idx, pad_start):
    T = idx.shape[0]
    tm = 512
    t_idx = jnp.arange(tm)
    tril = (t_idx[:, None] > t_idx[None, :]).astype(BF16)
    const = lambda i: (0, 0)
    return pl.pallas_call(
        _route_kernel,
        grid=(T // tm,),
        in_specs=[pl.BlockSpec((tm, TOP_K), lambda i: (i, 0)),
                  pl.BlockSpec((1, EXPERT_LANES), const),
                  pl.BlockSpec((tm, tm), const)],
        out_specs=pl.BlockSpec((tm, TOP_K), lambda i: (i, 0)),
        out_shape=jax.ShapeDtypeStruct((T, TOP_K), jnp.int32),
        scratch_shapes=[pltpu.VMEM((1, EXPERT_LANES), F32)],
        compiler_params=_cparams("arbitrary"),
        name="moe_route",
    )(idx, pad_start, tril)


DMA_ROWS_PER_STEP = 4
BLOCK_TILE_ROWS = MOE_BLOCK * ROW_SUBLANES


def _start_row_copies(make_copy, n_rows):
    def issue(g, _):
        for u in range(DMA_ROWS_PER_STEP):
            for kk in range(TOP_K):
                make_copy(g * DMA_ROWS_PER_STEP + u, kk).start(priority=kk % 2)
        return 0

    lax.fori_loop(0, n_rows // DMA_ROWS_PER_STEP, issue, 0)


def _wait_row_copies(make_copy, n_rows):
    def drain(g, _):
        for u in range(DMA_ROWS_PER_STEP):
            for kk in range(TOP_K):
                make_copy(g * DMA_ROWS_PER_STEP + u, kk).wait()
        return 0

    lax.fori_loop(0, n_rows // DMA_ROWS_PER_STEP, drain, 0)


def _row_tile(ref, r):
    return ref.at[pl.ds(pl.multiple_of(r * ROW_SUBLANES, ROW_SUBLANES), ROW_SUBLANES), :]


def _scatter_kernel(zero_ref, dest_ref, x_ref, slots_ref, zbuf_ref, sem, zsem):
    @pl.when(pl.program_id(0) == 0)
    def _():
        zbuf_ref[...] = jnp.zeros_like(zbuf_ref)

        def zero_copy(z):
            first = pl.multiple_of(zero_ref[z] * BLOCK_TILE_ROWS, BLOCK_TILE_ROWS)
            return pltpu.make_async_copy(zbuf_ref, slots_ref.at[pl.ds(first, BLOCK_TILE_ROWS), :], zsem)

        for z in range(zero_ref.shape[0]):
            @pl.when(zero_ref[z] >= 0)
            def _():
                zero_copy(z).start()
        for z in range(zero_ref.shape[0]):
            @pl.when(zero_ref[z] >= 0)
            def _():
                zero_copy(z).wait()

    def row_copy(t, kk):
        return pltpu.make_async_copy(_row_tile(x_ref, t), _row_tile(slots_ref, dest_ref[t * TOP_K + kk]), sem)

    n_rows = x_ref.shape[0] // ROW_SUBLANES
    _start_row_copies(row_copy, n_rows)
    _wait_row_copies(row_copy, n_rows)


def _scatter_rows(xn_tiles, dest_flat, zero_blocks, n_slots):
    T = xn_tiles.shape[0] // ROW_SUBLANES
    tm = 256
    grid_spec = pltpu.PrefetchScalarGridSpec(
        num_scalar_prefetch=1,
        grid=(T // tm,),
        in_specs=[pl.BlockSpec((tm * TOP_K,), lambda i, z: (i,), memory_space=pltpu.SMEM),
                  pl.BlockSpec((tm * ROW_SUBLANES, LANES), lambda i, z: (i, 0))],
        out_specs=pl.BlockSpec(memory_space=pl.ANY),
        scratch_shapes=[pltpu.VMEM((BLOCK_TILE_ROWS, LANES), F32),
                        pltpu.SemaphoreType.DMA(()), pltpu.SemaphoreType.DMA(())],
    )
    return pl.pallas_call(
        _scatter_kernel,
        grid_spec=grid_spec,
        out_shape=jax.ShapeDtypeStruct((n_slots * ROW_SUBLANES, LANES), F32),
        compiler_params=_cparams("arbitrary"),
        name="moe_scatter",
    )(zero_blocks, dest_flat, xn_tiles)


def _expert_kernel(be_ref, nu_ref, x_ref, wg_ref, bg_ref, wu_ref, bu_ref, wd_ref, bd_ref, y_ref,
                   wg_bf, wu_bf, wd_bf):
    i = pl.program_id(0)
    used = i < nu_ref[0]

    @pl.when(jnp.logical_not(used))
    def _():
        y_ref[...] = jnp.zeros_like(y_ref)

    @pl.when(jnp.logical_and(used, jnp.logical_or(i == 0, be_ref[i] != be_ref[jnp.maximum(i - 1, 0)])))
    def _():
        wg_bf[...] = wg_ref[0].astype(BF16)
        wu_bf[...] = wu_ref[0].astype(BF16)
        wd_bf[...] = wd_ref[0].astype(BF16)

    @pl.when(used)
    def _():
        xb = _load_row_tiles(x_ref, MOE_BLOCK).astype(BF16)
        gate = jnp.minimum(_dot(xb, wg_bf[...]) + bg_ref[0], SWIGLU_LIMIT)
        up = jnp.clip(_dot(xb, wu_bf[...]) + bu_ref[0], -SWIGLU_LIMIT, SWIGLU_LIMIT)
        glu = gate * jax.nn.sigmoid(SWIGLU_ALPHA * gate)
        _store_row_tiles(y_ref, _dot(((up + 1.0) * glu).astype(BF16), wd_bf[...]) + bd_ref[0])


def _expert_ffn(slots, block_expert, n_used, w_gate, b_gate, w_up, b_up, w_down, b_down):
    E, D, F = w_gate.shape
    n_blocks = slots.shape[0] // BLOCK_TILE_ROWS
    row = lambda i, be, nu: (jnp.minimum(i, nu[0] - 1), 0)
    wsel = lambda i, be, nu: (be[i], 0, 0)
    grid_spec = pltpu.PrefetchScalarGridSpec(
        num_scalar_prefetch=2,
        grid=(n_blocks,),
        in_specs=[pl.BlockSpec((BLOCK_TILE_ROWS, LANES), row),
                  pl.BlockSpec((1, D, F), wsel), pl.BlockSpec((1, 1, F), wsel),
                  pl.BlockSpec((1, D, F), wsel), pl.BlockSpec((1, 1, F), wsel),
                  pl.BlockSpec((1, F, D), wsel), pl.BlockSpec((1, 1, D), wsel)],
        out_specs=pl.BlockSpec((BLOCK_TILE_ROWS, LANES), lambda i, be, nu: (i, 0)),
        scratch_shapes=[pltpu.VMEM((D, F), BF16), pltpu.VMEM((D, F), BF16), pltpu.VMEM((F, D), BF16)],
    )
    return pl.pallas_call(
        _expert_kernel,
        grid_spec=grid_spec,
        out_shape=jax.ShapeDtypeStruct(slots.shape, F32),
        compiler_params=pltpu.CompilerParams(dimension_semantics=("arbitrary",),
                                             vmem_limit_bytes=EXPERT_VMEM_LIMIT),
        name="moe_experts",
    )(block_expert, n_used, slots, w_gate, b_gate.reshape(E, 1, F), w_up,
      b_up.reshape(E, 1, F), w_down, b_down.reshape(E, 1, D))


def _combine_kernel(dest_ref, dest_next_ref, h_ref, gate_ref, g_ref, y_ref, o_ref, buf_ref, sem):
    i = pl.program_id(0)
    tm = h_ref.shape[0]
    slot = i % 2

    def gather(dref, s):
        def row_copy(t, kk):
            return pltpu.make_async_copy(_row_tile(y_ref, dref[t * TOP_K + kk]),
                                         _row_tile(buf_ref.at[s, kk], t), sem.at[s])
        return row_copy

    @pl.when(i == 0)
    def _():
        _start_row_copies(gather(dest_ref, 0), tm)

    @pl.when(i + 1 < pl.num_programs(0))
    def _():
        _start_row_copies(gather(dest_next_ref, 1 - slot), tm)

    _wait_row_copies(gather(dest_ref, slot), tm)
    gates = gate_ref[...]
    h = h_ref[...]
    for kk in range(TOP_K):
        h = h + gates[:, kk:kk + 1] * _load_row_tiles(buf_ref.at[slot, kk], tm)
    o_ref[...] = h * lax.rsqrt(jnp.mean(h * h, axis=-1, keepdims=True) + NORM_EPS) * g_ref[...]


def _combine(h1, gates, dest_flat, y_slots, norm_g):
    T, D = h1.shape
    tm = 256
    n_tiles = T // tm
    return pl.pallas_call(
        _combine_kernel,
        grid=(n_tiles,),
        in_specs=[pl.BlockSpec((tm * TOP_K,), lambda i: (i,), memory_space=pltpu.SMEM),
                  pl.BlockSpec((tm * TOP_K,), lambda i: (jnp.minimum(i + 1, n_tiles - 1),),
                               memory_space=pltpu.SMEM),
                  pl.BlockSpec((tm, D), lambda i: (i, 0)),
                  pl.BlockSpec((tm, TOP_K), lambda i: (i, 0)),
                  pl.BlockSpec((1, D), lambda i: (0, 0)),
                  pl.BlockSpec(memory_space=pl.ANY)],
        out_specs=pl.BlockSpec((tm, D), lambda i: (i, 0)),
        out_shape=jax.ShapeDtypeStruct((T, D), F32),
        scratch_shapes=[pltpu.VMEM((2, TOP_K, tm * ROW_SUBLANES, LANES), F32),
                        pltpu.SemaphoreType.DMA((2,))],
        compiler_params=_cparams("arbitrary"),
        name="moe_combine",
    )(dest_flat, dest_flat, h1, gates, norm_g.reshape(1, D), y_slots)


def _moe(h1, xn2, idx, gates, counts, w_gate, b_gate, w_up, b_up, w_down, b_down, norm_final_g):
    T, D = h1.shape
    E = w_gate.shape[0]
    n_blocks = T * TOP_K // MOE_BLOCK + E
    cnt = counts[0].astype(jnp.int32)
    padded = (cnt + MOE_BLOCK - 1) // MOE_BLOCK * MOE_BLOCK
    pad_end = jnp.cumsum(padded)
    pad_start = (pad_end - padded).astype(F32).reshape(1, EXPERT_LANES)
    block_first = jnp.arange(n_blocks, dtype=jnp.int32) * MOE_BLOCK
    block_expert = jnp.minimum(
        jnp.sum((pad_end[None, :E] <= block_first[:, None]).astype(jnp.int32), axis=1), E - 1)
    n_used = (pad_end[E - 1] // MOE_BLOCK).astype(jnp.int32).reshape(1)
    last_block = jnp.where(padded[:E] > 0, pad_end[:E] // MOE_BLOCK - 1, -1)
    spare = n_used[0] + jnp.arange(E, dtype=jnp.int32)
    zero_blocks = jnp.concatenate([last_block, jnp.where(spare < n_blocks, spare, -1)]).astype(jnp.int32)
    dest = _route(idx, pad_start).reshape(T * TOP_K)
    slots = _scatter_rows(xn2, dest, zero_blocks, n_blocks * MOE_BLOCK)
    y_slots = _expert_ffn(slots, block_expert, n_used, w_gate, b_gate, w_up, b_up, w_down, b_down)
    return _combine(h1, gates, dest, y_slots, norm_final_g)


def kernel(x, norm_mix_g, w_in, attn_norm_g, rwkv_mu, rwkv_w0, rwkv_w2, rwkv_a0, rwkv_a2,
           rwkv_g2, rwkv_k_k, rwkv_k_a, rwkv_r_k, rwkv_ln_g, rwkv_ln_b, w_out, norm_ffn_g,
           w_router, b_router, moe_w_gate, moe_b_gate, moe_w_up, moe_b_up, moe_w_down,
           moe_b_down, norm_final_g):
    B, S, D = x.shape
    x2 = x.reshape(B * S, D)
    q, k, vt, km, xs = _in_projection(x2, norm_mix_g[0], w_in[0], rwkv_mu[0], B, S)
    o_att = _moba_attention(q, k, vt, km, attn_norm_g[0], B, S)
    o_rwkv = _rwkv_mix(xs, rwkv_w0[0], rwkv_w2[0], rwkv_a0[0], rwkv_a2[0], rwkv_g2[0], rwkv_k_k[0],
                       rwkv_k_a[0], rwkv_r_k[0], rwkv_ln_g[0], rwkv_ln_b[0], B, S)
    h1, xn2, idx, gates, counts = _outproj_router(o_att, o_rwkv, x2, w_out[0], norm_ffn_g[0],
                                                  w_router[0], b_router[0])
    out = _moe(h1, xn2, idx, gates, counts, moe_w_gate[0], moe_b_gate[0], moe_w_up[0], moe_b_up[0],
               moe_w_down[0], moe_b_down[0], norm_final_g)
    return out.reshape(B, S, D)
```

```python
import functools
import math

import jax
import jax.numpy as jnp
from jax import lax
from jax.experimental import pallas as pl
from jax.experimental.pallas import tpu as pltpu

F32 = jnp.float32
BF16 = jnp.bfloat16

HEAD_DIM = 64
N_HEADS = 8
GROUP_WIDTH = N_HEADS * HEAD_DIM
ATTN_TILE_WIDTH = N_HEADS * 2 * HEAD_DIM
V_ROWS = HEAD_DIM + 16
LOG2E = math.log2(math.e)
Q_SCALE = HEAD_DIM ** -0.5 * LOG2E
MOBA_BLOCK = 256
MOBA_TOPK = 3
N_EXPERTS = 32
TOP_K = 4
MOE_BLOCK = 512
SWIGLU_LIMIT = 7.0
SWIGLU_ALPHA = 1.702
NORM_EPS = 1e-6
NEG_INF = -1e30
RWKV_DECAY_SCALE = math.exp(-0.5)
RWKV_GN_EPS = HEAD_DIM * 1e-5
RWKV_LORA = (32, 32, 96)
RWKV_LORA_PAD = 256
RWKV_CHUNK = 64
RWKV_CHUNKS_PER_STEP = 4

VMEM_LIMIT = 48 * 1024 * 1024
EXPERT_VMEM_LIMIT = 56 * 1024 * 1024


def _cparams(*sem):
    return pltpu.CompilerParams(dimension_semantics=sem, vmem_limit_bytes=VMEM_LIMIT)


def _dot(a, b):
    return jnp.dot(a, b, preferred_element_type=F32)


def _dot_nt(a, b):
    return lax.dot_general(a, b, (((1,), (1,)), ((), ())), preferred_element_type=F32)


def _dot_tn(a, b):
    return lax.dot_general(a, b, (((0,), (0,)), ((), ())), preferred_element_type=F32)


def _split2(x):
    hi = x.astype(BF16)
    lo = (x - hi.astype(F32)).astype(BF16)
    return hi, lo


def _one_head_per_tile(x):
    pair = 2 * HEAD_DIM
    low = lax.broadcasted_iota(jnp.int32, (x.shape[0], pair), 1) < HEAD_DIM
    tiles = []
    for h in range(N_HEADS):
        src = x[:, (h // 2) * pair:(h // 2 + 1) * pair]
        if h % 2:
            src = pltpu.roll(src, HEAD_DIM, 1)
        tiles.append(jnp.where(low, src, 0.0))
    return jnp.concatenate(tiles, axis=1)


def _inproj_kernel(x_ref, g_ref, wq_ref, wk_ref, wvt_ref, wr_ref, mu_ref, qaug_ref, kaug_ref, vaug_ref,
                   q_ref, k_ref, vt_ref, kmean_ref, xs_ref, carry_ref, *, tiles_per_seq):
    i = pl.program_id(0)
    x = x_ref[...]
    xn = x * lax.rsqrt(jnp.mean(x * x, axis=-1, keepdims=True) + NORM_EPS) * g_ref[...]
    xb = xn.astype(BF16)
    q_ref[...] = (_one_head_per_tile(_dot(xb, wq_ref[...])) * Q_SCALE + qaug_ref[...]).astype(BF16)
    k = _one_head_per_tile(_dot(xb, wk_ref[...]))
    k_ref[...] = (k + kaug_ref[...]).astype(BF16)
    kmean_ref[0] = jnp.mean(k, axis=0, keepdims=True)
    vt_ref[0] = (_dot_nt(wvt_ref[...], xb) + vaug_ref[...]).astype(BF16)
    pr = _dot(xb, wr_ref[...])

    @pl.when(i % tiles_per_seq == 0)
    def _():
        carry_ref[...] = jnp.zeros_like(carry_ref)

    tm = pr.shape[0]
    rolled = pltpu.roll(pr, 1, 0)
    row = lax.broadcasted_iota(jnp.int32, pr.shape, 0)
    prev = jnp.where(row == 0, carry_ref[...], rolled)
    carry_ref[...] = pr[tm - 1:tm, :]
    xs_ref[...] = pr + (prev - pr) * mu_ref[...]


def _in_projection(x2, norm_g, w_in, mu, batch, seq):
    T, D = x2.shape
    tm = MOBA_BLOCK
    nb = seq // tm
    GW = GROUP_WIDTH
    AW = ATTN_TILE_WIDTH

    wq = w_in[:, :GW].astype(BF16)
    wk = w_in[:, GW:2 * GW].astype(BF16)
    pos = jnp.arange(tm, dtype=F32)[:, None, None]
    slope = jnp.asarray(_alibi_slopes(), F32)[None, :, None]
    lane = jnp.arange(2 * HEAD_DIM)[None, None, :] - HEAD_DIM
    bias = slope * pos * LOG2E
    b_hi = lax.bitcast_convert_type(
        lax.bitcast_convert_type(bias, jnp.uint32) & jnp.uint32(0xFFFF0000), F32)
    b_lo = bias - b_hi
    pick = lambda l0, l1, l2, l3: jnp.where(lane == 0, l0, jnp.where(lane == 1, l1, jnp.where(
        lane == 2, l2, jnp.where(lane == 3, l3, 0.0)))).reshape(tm, AW)
    qaug = pick(-b_hi, -b_lo, 1.0, 1.0)
    kaug = pick(1.0, 1.0, b_hi, b_lo)
    wvt = jnp.pad(w_in[:, 2 * GW:3 * GW].T.reshape(N_HEADS, HEAD_DIM, D),
                  ((0, 0), (0, V_ROWS - HEAD_DIM), (0, 0))).reshape(N_HEADS * V_ROWS, D).astype(BF16)
    vaug = jnp.broadcast_to((jnp.arange(N_HEADS * V_ROWS) % V_ROWS == HEAD_DIM).astype(F32)[:, None],
                            (N_HEADS * V_ROWS, tm))
    rw = 3 * GW + RWKV_LORA_PAD
    n_r = w_in.shape[1] - 3 * GW
    wr = jnp.pad(w_in[:, 3 * GW:], ((0, 0), (0, rw - n_r))).astype(BF16)
    mu_p = jnp.pad(mu, (0, rw - n_r)).reshape(1, rw)
    const = lambda i: (0, 0)
    return pl.pallas_call(
        functools.partial(_inproj_kernel, tiles_per_seq=nb),
        grid=(T // tm,),
        in_specs=[
            pl.BlockSpec((tm, D), lambda i: (i, 0)),
            pl.BlockSpec((1, D), const),
            pl.BlockSpec((D, GW), const),
            pl.BlockSpec((D, GW), const),
            pl.BlockSpec((N_HEADS * V_ROWS, D), const),
            pl.BlockSpec((D, rw), const),
            pl.BlockSpec((1, rw), const),
            pl.BlockSpec((tm, AW), const),
            pl.BlockSpec((tm, AW), const),
            pl.BlockSpec((N_HEADS * V_ROWS, tm), const),
        ],
        out_specs=[
            pl.BlockSpec((tm, AW), lambda i: (i, 0)),
            pl.BlockSpec((tm, AW), lambda i: (i, 0)),
            pl.BlockSpec((1, N_HEADS * V_ROWS, tm), lambda i: (i, 0, 0)),
            pl.BlockSpec((1, 1, AW), lambda i: (i, 0, 0)),
            pl.BlockSpec((tm, rw), lambda i: (i, 0)),
        ],
        out_shape=[
            jax.ShapeDtypeStruct((T, AW), BF16),
            jax.ShapeDtypeStruct((T, AW), BF16),
            jax.ShapeDtypeStruct((T // tm, N_HEADS * V_ROWS, tm), BF16),
            jax.ShapeDtypeStruct((T // tm, 1, AW), F32),
            jax.ShapeDtypeStruct((T, rw), F32),
        ],
        scratch_shapes=[pltpu.VMEM((1, rw), F32)],
        compiler_params=_cparams("arbitrary"),
        name="in_projection",
    )(x2, norm_g.reshape(1, D), wq, wk, wvt, wr, mu_p, qaug, kaug, vaug)


def _moba_kernel(q_ref, k_ref, vt_ref, kmean_ref, g_ref,
                 o_ref, selb_ref, m_ref, acc_ref, ot_ref, *, nb, nbp, slopes):
    i = pl.program_id(1)
    blk = MOBA_BLOCK
    heads = range(N_HEADS)
    tile = [slice(h * 2 * HEAD_DIM, (h + 1) * 2 * HEAD_DIM) for h in heads]
    vrow = [slice(h * V_ROWS, (h + 1) * V_ROWS) for h in heads]
    n_iota = lax.broadcasted_iota(jnp.int32, (nbp, blk), 0)
    valid = n_iota < i

    gates = []
    for h in heads:
        km_hi, km_lo = _split2(kmean_ref[0, :, tile[h]])
        qh = q_ref[:, tile[h]]
        gates.append(jnp.where(valid, _dot_nt(km_hi, qh) + _dot_nt(km_lo, qh), NEG_INF))
    for h in heads:
        rank = jnp.zeros((nbp, blk), jnp.int32)
        for m in range(nb):
            gm = gates[h][m:m + 1, :]
            tie = (n_iota > m).astype(jnp.int32)
            rank = rank + jnp.where(gm > gates[h], 1, jnp.where(gm == gates[h], tie, 0))
        selb_ref[h] = jnp.where(rank < MOBA_TOPK, jnp.where(valid, 0.0, NEG_INF), NEG_INF)

    own = pl.ds(pl.multiple_of(i * blk, blk), blk)
    future = (lax.broadcasted_iota(jnp.int32, (blk, blk), 0) >
              lax.broadcasted_iota(jnp.int32, (blk, blk), 1))
    st = [jnp.where(future, NEG_INF, _dot_nt(k_ref[0, own, tile[h]], q_ref[:, tile[h]])) for h in heads]
    ps = []
    for h in heads:
        m0 = jnp.max(st[h], axis=0, keepdims=True)
        m_ref[h] = m0
        ps.append(jnp.exp2(st[h] - m0).astype(BF16))
    for h in heads:
        acc_ref[vrow[h], :] = _dot(vt_ref[0, i, vrow[h], :], ps[h])

    def past_blocks(js):
        st = [[_dot_nt(k_ref[0, pl.ds(pl.multiple_of(j * blk, blk), blk), tile[h]], q_ref[:, tile[h]])
               for h in heads] for j in js]
        ps, alphas = [], []
        for h in heads:
            offs = [selb_ref[h, pl.ds(j, 1), :] - slopes[h] * ((i - j).astype(F32) * (blk * LOG2E)) for j in js]
            m_old = m_ref[h]
            m_new = m_old
            for a in range(len(js)):
                m_new = jnp.maximum(m_new, jnp.max(st[a][h], axis=0, keepdims=True) + offs[a])
            m_ref[h] = m_new
            ps.append([jnp.exp2(st[a][h] - (m_new - offs[a])).astype(BF16) for a in range(len(js))])
            alphas.append(jnp.exp2(m_old - m_new))
        for h in heads:
            pv = _dot(vt_ref[0, js[0], vrow[h], :], ps[h][0])
            for a in range(1, len(js)):
                pv = pv + _dot(vt_ref[0, js[a], vrow[h], :], ps[h][a])
            acc_ref[vrow[h], :] = alphas[h] * acc_ref[vrow[h], :] + pv

    def pair(p, _):
        past_blocks([2 * p, 2 * p + 1])
        return 0

    lax.fori_loop(0, i // 2, pair, 0)

    @pl.when(i % 2 == 1)
    def _():
        past_blocks([i - 1])

    for h in heads:
        v0 = h * V_ROWS
        o = acc_ref[v0:v0 + HEAD_DIM, :] / acc_ref[v0 + HEAD_DIM:v0 + HEAD_DIM + 1, :]
        ot_ref[h * HEAD_DIM:(h + 1) * HEAD_DIM, :] = o * lax.rsqrt(
            jnp.mean(o * o, axis=0, keepdims=True) + NORM_EPS)
    o_ref[...] = (ot_ref[...].T * g_ref[...]).astype(o_ref.dtype)


def _alibi_slopes():
    return [2.0 ** (-8.0 * (h + 1) / N_HEADS) for h in range(N_HEADS)]


def _moba_attention(q, k, vt, kmean, attn_norm_g, batch, seq):
    GW = GROUP_WIDTH
    AW = ATTN_TILE_WIDTH
    blk = MOBA_BLOCK
    nb = seq // blk
    nbp = -(-nb // 16) * 16
    k3 = k.reshape(batch, seq, AW)
    VW = N_HEADS * V_ROWS
    vt4 = vt.reshape(batch, nb, VW, blk)
    km = jnp.pad(kmean.reshape(batch, nb, AW), ((0, 0), (0, nbp - nb), (0, 0)))
    return pl.pallas_call(
        functools.partial(_moba_kernel, nb=nb, nbp=nbp, slopes=_alibi_slopes()),
        grid=(batch, nb),
        in_specs=[
            pl.BlockSpec((blk, AW), lambda b, i: (b * nb + i, 0)),
            pl.BlockSpec((1, seq, AW), lambda b, i: (b, 0, 0)),
            pl.BlockSpec((1, nb, VW, blk), lambda b, i: (b, 0, 0, 0)),
            pl.BlockSpec((1, nbp, AW), lambda b, i: (b, 0, 0)),
            pl.BlockSpec((1, GW), lambda b, i: (0, 0)),
        ],
        out_specs=pl.BlockSpec((blk, GW), lambda b, i: (b * nb + i, 0)),
        out_shape=jax.ShapeDtypeStruct((batch * seq, GW), BF16),
        scratch_shapes=[pltpu.VMEM((N_HEADS, nbp, blk), F32),
                        pltpu.VMEM((N_HEADS, 1, blk), F32),
                        pltpu.VMEM((VW, blk), F32),
                        pltpu.VMEM((GW, blk), F32)],
        compiler_params=_cparams("arbitrary", "arbitrary"),
        name="moba_attention",
    )(q, k3, vt4, km, attn_norm_g.reshape(1, GW))


def _mm(a, b, dims=((1,), (0,)), passes=3):
    dn = (dims, ((), ()))
    dg = lambda u, v: lax.dot_general(u, v, dn, preferred_element_type=F32)
    if passes == 1:
        return dg(a.astype(BF16), b.astype(BF16))
    a_hi, a_lo = _split2(a)
    b_hi, b_lo = _split2(b)
    return dg(a_hi, b_hi) + dg(a_hi, b_lo) + dg(a_lo, b_hi)


_NT = ((1,), (1,))
_TN = ((0,), (0,))
RWKV_PASSES = {"scores": 1, "inverse": 1, "outer": 1, "state": 1}


def _head_sum(x, pair_ones):
    x_hi, x_lo = _split2(x)
    w = pair_ones.shape[0]
    tiles = [_dot(x_hi[:, t:t + w], pair_ones) + _dot(x_lo[:, t:t + w], pair_ones)
             for t in range(0, x.shape[1], w)]
    return jnp.concatenate(tiles, axis=1)


def _rwkv_kernel(xs_ref, w2_ref, a2_ref, g2_ref, vec_ref, tril_ref, hsum_ref,
                 o_ref, state_ref, cum_ref, lw_ref, r_ref, kk_ref, b_ref, k2_ref, v_ref, oc_ref):
    GW = GROUP_WIDTH
    L = RWKV_CHUNK
    N = HEAD_DIM
    tm = xs_ref.shape[0]
    w0, a0, k_k, k_a, r_k, ln_g, ln_b = [vec_ref[n:n + 1, :] for n in range(7)]

    @pl.when(pl.program_id(1) == 0)
    def _():
        state_ref[...] = jnp.zeros_like(state_ref)

    r = xs_ref[:, 0:GW]
    k = xs_ref[:, GW:2 * GW]
    v = xs_ref[:, 2 * GW:3 * GW]
    lo = xs_ref[:, 3 * GW:]
    hsum = hsum_ref[...]
    logw = -RWKV_DECAY_SCALE * jax.nn.sigmoid(w0 + _dot(jnp.tanh(lo).astype(BF16), w2_ref[...]))
    a = jax.nn.sigmoid(a0 + _dot(lo.astype(BF16), a2_ref[...]))
    gate = _dot(jax.nn.sigmoid(lo).astype(BF16), g2_ref[...])
    kk = k * k_k
    kk = kk * lax.rsqrt(jnp.maximum(_head_sum(kk * kk, hsum), 1e-24))
    k2 = k * (1.0 + (a - 1.0) * k_a)
    bonus = _head_sum(r * k2 * r_k, hsum) * v
    lw_hi, lw_lo = _split2(logw)
    cum_ref[...] = _dot(tril_ref[...], lw_hi) + _dot(tril_ref[...], lw_lo)
    lw_ref[...] = logw
    r_ref[...] = r
    kk_ref[...] = kk
    b_ref[...] = kk * a
    k2_ref[...] = k2
    v_ref[...] = v

    row = lax.broadcasted_iota(jnp.int32, (L, L), 0)
    col = lax.broadcasted_iota(jnp.int32, (L, L), 1)
    strict = row > col
    incl = row >= col
    eye = row == col

    U = RWKV_CHUNKS_PER_STEP
    hsl = [slice(h * N, (h + 1) * N) for h in range(N_HEADS)]
    units = [(u, h) for u in range(U) for h in range(N_HEADS)]

    def step(c, _):
        pre = []
        for u in range(U):
            rows = pl.ds(pl.multiple_of((c * U + u) * L, L), L)
            cum = cum_ref[rows, :]
            cum_end = cum[L - 1:L, :]
            w_incl = jnp.exp(cum)
            w_excl = jnp.exp(cum - lw_ref[rows, :])
            w_inv = jnp.exp(-cum)
            e_end = jnp.exp(cum_end - cum)
            pre.append(dict(
                rows=rows, w_end=jnp.exp(cum_end),
                rhat=r_ref[rows, :] * w_incl, kkhat=kk_ref[rows, :] * w_excl,
                btil=b_ref[rows, :] * w_inv, ktil=k2_ref[rows, :] * w_inv,
                bbar=b_ref[rows, :] * e_end, kbar=k2_ref[rows, :] * e_end, v=v_ref[rows, :]))
        sl = lambda name, u, h: pre[u][name][:, hsl[h]]
        ma = [_mm(jnp.concatenate([sl("kkhat", u, h), sl("rhat", u, h)], axis=0),
                  jnp.concatenate([sl("btil", u, h), sl("ktil", u, h)], axis=0), _NT, RWKV_PASSES["scores"])
              for u, h in units]
        m_bk = [jnp.where(strict, m[:L, :L], 0.0) for m in ma]
        m_kk = [jnp.where(strict, m[:L, L:], 0.0) for m in ma]
        a_br = [jnp.where(incl, m[L:, :L], 0.0) for m in ma]
        a_kr = [jnp.where(incl, m[L:, L:], 0.0) for m in ma]
        n_u = range(len(units))
        vh = [sl("v", u, h) for u, h in units]
        pi = RWKV_PASSES["inverse"]
        y = [jnp.concatenate([sl("kkhat", *units[n]), _mm(m_kk[n], vh[n], passes=pi)], axis=1) for n in n_u]
        y = [y[n] - _mm(m_bk[n], y[n], passes=pi) for n in n_u]
        pw = m_bk
        for _ in range(int(math.log2(L)) - 1):
            pw = [_mm(pw[n], pw[n], passes=pi) for n in n_u]
            y = [y[n] + _mm(pw[n], y[n], passes=pi) for n in n_u]
        po = RWKV_PASSES["outer"]
        xtb = [_mm(y[n], sl("bbar", *units[n]), _TN, po) for n in n_u]
        vtk = [_mm(vh[n], sl("kbar", *units[n]), _TN, po) for n in n_u]
        ax = [_mm(a_br[n], y[n], passes=po) for n in n_u]
        akv = [_mm(a_kr[n], vh[n], passes=po) for n in n_u]
        ps = RWKV_PASSES["state"]
        state = [state_ref[h] for h in range(N_HEADS)]
        for n, (u, h) in enumerate(units):
            r2 = sl("rhat", u, h) - ax[n][:, :N]
            oc_ref[pre[u]["rows"], hsl[h]] = _mm(r2, state[h], _NT, ps) + akv[n] - ax[n][:, N:]
            p_mat = jnp.where(eye, pre[u]["w_end"][:, hsl[h]], 0.0) - xtb[n][:N]
            state[h] = _mm(state[h], p_mat, passes=ps) + vtk[n] - xtb[n][N:]
        for h in range(N_HEADS):
            state_ref[h] = state[h]
        return 0

    lax.fori_loop(0, tm // (L * U), step, 0)

    o = oc_ref[...]
    mean = _head_sum(o, hsum) * (1.0 / N)
    d = o - mean
    var = _head_sum(d * d, hsum) * (1.0 / N)
    o = d * lax.rsqrt(var + RWKV_GN_EPS) * ln_g + ln_b
    o_ref[...] = ((o + bonus) * gate).astype(o_ref.dtype)


def _rwkv_mix(xs, w0, w2, a0, a2, g2, k_k, k_a, r_k, ln_g, ln_b, batch, seq):
    T, rw = xs.shape
    GW = GROUP_WIDTH
    L = RWKV_CHUNK
    tm = 256
    nl = rw - 3 * GW
    d0, d1, d2 = RWKV_LORA
    w2p = jnp.pad(w2, ((0, nl - d0), (0, 0))).astype(BF16)
    a2p = jnp.pad(a2, ((d0, nl - d0 - d1), (0, 0))).astype(BF16)
    g2p = jnp.pad(g2, ((d0 + d1, nl - d0 - d1 - d2), (0, 0))).astype(BF16)
    vecs = jnp.stack([w0, a0, k_k, k_a, r_k.reshape(GW), ln_g, ln_b, jnp.zeros((GW,), F32)])
    t_idx = jnp.arange(tm)
    tril = ((t_idx[:, None] >= t_idx[None, :]) & (t_idx[:, None] // L == t_idx[None, :] // L)).astype(BF16)
    c_idx = jnp.arange(2 * HEAD_DIM) // HEAD_DIM
    hsum = (c_idx[:, None] == c_idx[None, :]).astype(BF16)
    tps = seq // tm
    const = lambda b, i: (0, 0)
    scr = lambda: pltpu.VMEM((tm, GW), F32)
    return pl.pallas_call(
        _rwkv_kernel,
        grid=(batch, tps),
        in_specs=[
            pl.BlockSpec((tm, rw), lambda b, i: (b * tps + i, 0)),
            pl.BlockSpec((nl, GW), const),
            pl.BlockSpec((nl, GW), const),
            pl.BlockSpec((nl, GW), const),
            pl.BlockSpec((8, GW), const),
            pl.BlockSpec((tm, tm), const),
            pl.BlockSpec((2 * HEAD_DIM, 2 * HEAD_DIM), const),
        ],
        out_specs=pl.BlockSpec((tm, GW), lambda b, i: (b * tps + i, 0)),
        out_shape=jax.ShapeDtypeStruct((T, GW), BF16),
        scratch_shapes=[pltpu.VMEM((N_HEADS, HEAD_DIM, HEAD_DIM), F32)] + [scr() for _ in range(8)],
        compiler_params=_cparams("arbitrary", "arbitrary"),
        name="rwkv7_mix",
    )(xs, w2p, a2p, g2p, vecs, tril, hsum)


EXPERT_LANES = 128

ROW_SUBLANES = 8
LANES = 128


def _store_row_tiles(ref, x):
    n = x.shape[0]
    for s in range(ROW_SUBLANES):
        ref[pl.ds(s, n, stride=ROW_SUBLANES), :] = x[:, s * LANES:(s + 1) * LANES]


def _load_row_tiles(ref, n):
    return jnp.concatenate([ref[pl.ds(s, n, stride=ROW_SUBLANES), :] for s in range(ROW_SUBLANES)], axis=1)


def _outproj_router_kernel(oa_ref, or_ref, x_ref, wa_ref, wr_ref, g_ref, wrt_ref, brt_ref,
                           h_ref, xn_ref, idx_ref, gate_ref, cnt_ref):
    h = x_ref[...] + _dot(oa_ref[...], wa_ref[...]) + _dot(or_ref[...], wr_ref[...])
    h_ref[...] = h
    xn = h * lax.rsqrt(jnp.mean(h * h, axis=-1, keepdims=True) + NORM_EPS) * g_ref[...]
    _store_row_tiles(xn_ref, xn)
    logits = _mm(xn, wrt_ref[...]) + brt_ref[...]
    tm = logits.shape[0]
    lane = lax.broadcasted_iota(jnp.int32, logits.shape, 1)
    lane4 = lax.broadcasted_iota(jnp.int32, (tm, TOP_K), 1)
    idx_out = jnp.zeros((tm, TOP_K), jnp.int32)
    val_out = jnp.zeros((tm, TOP_K), F32)
    member = jnp.zeros(logits.shape, F32)
    top = None
    denom = jnp.zeros((tm, 1), F32)
    for kk in range(TOP_K):
        mx = jnp.max(logits, axis=-1, keepdims=True)
        idx = jnp.min(jnp.where(logits == mx, lane, EXPERT_LANES), axis=-1, keepdims=True)
        hit = lane == idx
        member = jnp.where(hit, 1.0, member)
        logits = jnp.where(hit, -jnp.inf, logits)
        top = mx if top is None else top
        e = jnp.exp(mx - top)
        denom = denom + e
        idx_out = jnp.where(lane4 == kk, idx, idx_out)
        val_out = jnp.where(lane4 == kk, e, val_out)
    idx_ref[...] = idx_out
    gate_ref[...] = val_out / denom

    @pl.when(pl.program_id(0) == 0)
    def _():
        cnt_ref[...] = jnp.zeros_like(cnt_ref)

    cnt_ref[...] += jnp.sum(member, axis=0, keepdims=True)


def _outproj_router(o_att, o_rwkv, x2, w_out, norm_g, w_router, b_router):
    T, D = x2.shape
    GW = GROUP_WIDTH
    tm = 1024
    E = w_router.shape[1]
    wa = w_out[:GW].astype(BF16)
    wr = w_out[GW:].astype(BF16)
    wrt = jnp.pad(w_router, ((0, 0), (0, EXPERT_LANES - E)))
    brt = jnp.pad(b_router, (0, EXPERT_LANES - E), constant_values=NEG_INF).reshape(1, EXPERT_LANES)
    const = lambda i: (0, 0)
    tile = lambda w: pl.BlockSpec((tm, w), lambda i: (i, 0))
    return pl.pallas_call(
        _outproj_router_kernel,
        grid=(T // tm,),
        in_specs=[tile(GW), tile(GW), tile(D),
                  pl.BlockSpec((GW, D), const), pl.BlockSpec((GW, D), const),
                  pl.BlockSpec((1, D), const), pl.BlockSpec((D, EXPERT_LANES), const),
                  pl.BlockSpec((1, EXPERT_LANES), const)],
        out_specs=[tile(D), pl.BlockSpec((tm * ROW_SUBLANES, LANES), lambda i: (i, 0)),
                   tile(TOP_K), tile(TOP_K), pl.BlockSpec((1, EXPERT_LANES), const)],
        out_shape=[jax.ShapeDtypeStruct((T, D), F32), jax.ShapeDtypeStruct((T * ROW_SUBLANES, LANES), F32),
                   jax.ShapeDtypeStruct((T, TOP_K), jnp.int32), jax.ShapeDtypeStruct((T, TOP_K), F32),
                   jax.ShapeDtypeStruct((1, EXPERT_LANES), F32)],
        compiler_params=_cparams("arbitrary"),
        name="outproj_router",
    )(o_att, o_rwkv, x2, wa, wr, norm_g.reshape(1, D), wrt, brt)


def _route_kernel(idx_ref, start_ref, tril_ref, dest_ref, carry_ref):
    @pl.when(pl.program_id(0) == 0)
    def _():
        carry_ref[...] = jnp.zeros_like(carry_ref)

    idx = idx_ref[...]
    tm = idx.shape[0]
    lane = lax.broadcasted_iota(jnp.int32, (tm, EXPERT_LANES), 1)
    lane4 = lax.broadcasted_iota(jnp.int32, (tm, TOP_K), 1)
    hits = [lane == idx[:, kk:kk + 1] for kk in range(TOP_K)]
    member = sum(jnp.where(hh, 1.0, 0.0) for hh in hits)
    before = _dot(tril_ref[...], member.astype(BF16)) + carry_ref[...]
    base = before + start_ref[...]
    dest = jnp.zeros((tm, TOP_K), jnp.int32)
    for kk in range(TOP_K):
        d = jnp.sum(jnp.where(hits[kk], base, 0.0), axis=-1, keepdims=True)
        dest = jnp.where(lane4 == kk, d.astype(jnp.int32), dest)
    dest_ref[...] = dest
    carry_ref[...] += jnp.sum(member, axis=0, keepdims=True)


def _route(idx, pad_start):
    T = idx.shape[0]
    tm = 512
    t_idx = jnp.arange(tm)
    tril = (t_idx[:, None] > t_idx[None, :]).astype(BF16)
    const = lambda i: (0, 0)
    return pl.pallas_call(
        _route_kernel,
        grid=(T // tm,),
        in_specs=[pl.BlockSpec((tm, TOP_K), lambda i: (i, 0)),
                  pl.BlockSpec((1, EXPERT_LANES), const),
                  pl.BlockSpec((tm, tm), const)],
        out_specs=pl.BlockSpec((tm, TOP_K), lambda i: (i, 0)),
        out_shape=jax.ShapeDtypeStruct((T, TOP_K), jnp.int32),
        scratch_shapes=[pltpu.VMEM((1, EXPERT_LANES), F32)],
        compiler_params=_cparams("arbitrary"),
        name="moe_route",
    )(idx, pad_start, tril)


DMA_ROWS_PER_STEP = 4
BLOCK_TILE_ROWS = MOE_BLOCK * ROW_SUBLANES


def _start_row_copies(make_copy, n_rows):
    def issue(g, _):
        for u in range(DMA_ROWS_PER_STEP):
            for kk in range(TOP_K):
                make_copy(g * DMA_ROWS_PER_STEP + u, kk).start(priority=kk % 2)
        return 0

    lax.fori_loop(0, n_rows // DMA_ROWS_PER_STEP, issue, 0)


def _wait_row_copies(make_copy, n_rows):
    def drain(g, _):
        for u in range(DMA_ROWS_PER_STEP):
            for kk in range(TOP_K):
                make_copy(g * DMA_ROWS_PER_STEP + u, kk).wait()
        return 0

    lax.fori_loop(0, n_rows // DMA_ROWS_PER_STEP, drain, 0)


def _row_tile(ref, r):
    return ref.at[pl.ds(pl.multiple_of(r * ROW_SUBLANES, ROW_SUBLANES), ROW_SUBLANES), :]


def _scatter_kernel(zero_ref, dest_ref, x_ref, slots_ref, zbuf_ref, sem, zsem):
    @pl.when(pl.program_id(0) == 0)
    def _():
        zbuf_ref[...] = jnp.zeros_like(zbuf_ref)

        def zero_copy(z):
            first = pl.multiple_of(zero_ref[z] * BLOCK_TILE_ROWS, BLOCK_TILE_ROWS)
            return pltpu.make_async_copy(zbuf_ref, slots_ref.at[pl.ds(first, BLOCK_TILE_ROWS), :], zsem)

        for z in range(zero_ref.shape[0]):
            @pl.when(zero_ref[z] >= 0)
            def _():
                zero_copy(z).start()
        for z in range(zero_ref.shape[0]):
            @pl.when(zero_ref[z] >= 0)
            def _():
                zero_copy(z).wait()

    def row_copy(t, kk):
        return pltpu.make_async_copy(_row_tile(x_ref, t), _row_tile(slots_ref, dest_ref[t * TOP_K + kk]), sem)

    n_rows = x_ref.shape[0] // ROW_SUBLANES
    _start_row_copies(row_copy, n_rows)
    _wait_row_copies(row_copy, n_rows)


def _scatter_rows(xn_tiles, dest_flat, zero_blocks, n_slots):
    T = xn_tiles.shape[0] // ROW_SUBLANES
    tm = 256
    grid_spec = pltpu.PrefetchScalarGridSpec(
        num_scalar_prefetch=1,
        grid=(T // tm,),
        in_specs=[pl.BlockSpec((tm * TOP_K,), lambda i, z: (i,), memory_space=pltpu.SMEM),
                  pl.BlockSpec((tm * ROW_SUBLANES, LANES), lambda i, z: (i, 0))],
        out_specs=pl.BlockSpec(memory_space=pl.ANY),
        scratch_shapes=[pltpu.VMEM((BLOCK_TILE_ROWS, LANES), F32),
                        pltpu.SemaphoreType.DMA(()), pltpu.SemaphoreType.DMA(())],
    )
    return pl.pallas_call(
        _scatter_kernel,
        grid_spec=grid_spec,
        out_shape=jax.ShapeDtypeStruct((n_slots * ROW_SUBLANES, LANES), F32),
        compiler_params=_cparams("arbitrary"),
        name="moe_scatter",
    )(zero_blocks, dest_flat, xn_tiles)


def _expert_kernel(be_ref, nu_ref, x_ref, wg_ref, bg_ref, wu_ref, bu_ref, wd_ref, bd_ref, y_ref,
                   wg_bf, wu_bf, wd_bf):
    i = pl.program_id(0)
    used = i < nu_ref[0]

    @pl.when(jnp.logical_not(used))
    def _():
        y_ref[...] = jnp.zeros_like(y_ref)

    @pl.when(jnp.logical_and(used, jnp.logical_or(i == 0, be_ref[i] != be_ref[jnp.maximum(i - 1, 0)])))
    def _():
        wg_bf[...] = wg_ref[0].astype(BF16)
        wu_bf[...] = wu_ref[0].astype(BF16)
        wd_bf[...] = wd_ref[0].astype(BF16)

    @pl.when(used)
    def _():
        xb = _load_row_tiles(x_ref, MOE_BLOCK).astype(BF16)
        gate = jnp.minimum(_dot(xb, wg_bf[...]) + bg_ref[0], SWIGLU_LIMIT)
        up = jnp.clip(_dot(xb, wu_bf[...]) + bu_ref[0], -SWIGLU_LIMIT, SWIGLU_LIMIT)
        glu = gate * jax.nn.sigmoid(SWIGLU_ALPHA * gate)
        _store_row_tiles(y_ref, _dot(((up + 1.0) * glu).astype(BF16), wd_bf[...]) + bd_ref[0])


def _expert_ffn(slots, block_expert, n_used, w_gate, b_gate, w_up, b_up, w_down, b_down):
    E, D, F = w_gate.shape
    n_blocks = slots.shape[0] // BLOCK_TILE_ROWS
    row = lambda i, be, nu: (jnp.minimum(i, nu[0] - 1), 0)
    wsel = lambda i, be, nu: (be[i], 0, 0)
    grid_spec = pltpu.PrefetchScalarGridSpec(
        num_scalar_prefetch=2,
        grid=(n_blocks,),
        in_specs=[pl.BlockSpec((BLOCK_TILE_ROWS, LANES), row),
                  pl.BlockSpec((1, D, F), wsel), pl.BlockSpec((1, 1, F), wsel),
                  pl.BlockSpec((1, D, F), wsel), pl.BlockSpec((1, 1, F), wsel),
                  pl.BlockSpec((1, F, D), wsel), pl.BlockSpec((1, 1, D), wsel)],
        out_specs=pl.BlockSpec((BLOCK_TILE_ROWS, LANES), lambda i, be, nu: (i, 0)),
        scratch_shapes=[pltpu.VMEM((D, F), BF16), pltpu.VMEM((D, F), BF16), pltpu.VMEM((F, D), BF16)],
    )
    return pl.pallas_call(
        _expert_kernel,
        grid_spec=grid_spec,
        out_shape=jax.ShapeDtypeStruct(slots.shape, F32),
        compiler_params=pltpu.CompilerParams(dimension_semantics=("arbitrary",),
                                             vmem_limit_bytes=EXPERT_VMEM_LIMIT),
        name="moe_experts",
    )(block_expert, n_used, slots, w_gate, b_gate.reshape(E, 1, F), w_up,
      b_up.reshape(E, 1, F), w_down, b_down.reshape(E, 1, D))


def _combine_kernel(dest_ref, dest_next_ref, h_ref, gate_ref, g_ref, y_ref, o_ref, buf_ref, sem):
    i = pl.program_id(0)
    tm = h_ref.shape[0]
    slot = i % 2

    def gather(dref, s):
        def row_copy(t, kk):
            return pltpu.make_async_copy(_row_tile(y_ref, dref[t * TOP_K + kk]),
                                         _row_tile(buf_ref.at[s, kk], t), sem.at[s])
        return row_copy

    @pl.when(i == 0)
    def _():
        _start_row_copies(gather(dest_ref, 0), tm)

    @pl.when(i + 1 < pl.num_programs(0))
    def _():
        _start_row_copies(gather(dest_next_ref, 1 - slot), tm)

    _wait_row_copies(gather(dest_ref, slot), tm)
    gates = gate_ref[...]
    h = h_ref[...]
    for kk in range(TOP_K):
        h = h + gates[:, kk:kk + 1] * _load_row_tiles(buf_ref.at[slot, kk], tm)
    o_ref[...] = h * lax.rsqrt(jnp.mean(h * h, axis=-1, keepdims=True) + NORM_EPS) * g_ref[...]


def _combine(h1, gates, dest_flat, y_slots, norm_g):
    T, D = h1.shape
    tm = 256
    n_tiles = T // tm
    return pl.pallas_call(
        _combine_kernel,
        grid=(n_tiles,),
        in_specs=[pl.BlockSpec((tm * TOP_K,), lambda i: (i,), memory_space=pltpu.SMEM),
                  pl.BlockSpec((tm * TOP_K,), lambda i: (jnp.minimum(i + 1, n_tiles - 1),),
                               memory_space=pltpu.SMEM),
                  pl.BlockSpec((tm, D), lambda i: (i, 0)),
                  pl.BlockSpec((tm, TOP_K), lambda i: (i, 0)),
                  pl.BlockSpec((1, D), lambda i: (0, 0)),
                  pl.BlockSpec(memory_space=pl.ANY)],
        out_specs=pl.BlockSpec((tm, D), lambda i: (i, 0)),
        out_shape=jax.ShapeDtypeStruct((T, D), F32),
        scratch_shapes=[pltpu.VMEM((2, TOP_K, tm * ROW_SUBLANES, LANES), F32),
                        pltpu.SemaphoreType.DMA((2,))],
        compiler_params=_cparams("arbitrary"),
        name="moe_combine",
    )(dest_flat, dest_flat, h1, gates, norm_g.reshape(1, D), y_slots)


def _moe(h1, xn2, idx, gates, counts, w_gate, b_gate, w_up, b_up, w_down, b_down, norm_final_g):
    T, D = h1.shape
    E = w_gate.shape[0]
    n_blocks = T * TOP_K // MOE_BLOCK + E
    cnt = counts[0].astype(jnp.int32)
    padded = (cnt + MOE_BLOCK - 1) // MOE_BLOCK * MOE_BLOCK
    pad_end = jnp.cumsum(padded)
    pad_start = (pad_end - padded).astype(F32).reshape(1, EXPERT_LANES)
    block_first = jnp.arange(n_blocks, dtype=jnp.int32) * MOE_BLOCK
    block_expert = jnp.minimum(
        jnp.sum((pad_end[None, :E] <= block_first[:, None]).astype(jnp.int32), axis=1), E - 1)
    n_used = (pad_end[E - 1] // MOE_BLOCK).astype(jnp.int32).reshape(1)
    last_block = jnp.where(padded[:E] > 0, pad_end[:E] // MOE_BLOCK - 1, -1)
    spare = n_used[0] + jnp.arange(E, dtype=jnp.int32)
    zero_blocks = jnp.concatenate([last_block, jnp.where(spare < n_blocks, spare, -1)]).astype(jnp.int32)
    dest = _route(idx, pad_start).reshape(T * TOP_K)
    slots = _scatter_rows(xn2, dest, zero_blocks, n_blocks * MOE_BLOCK)
    y_slots = _expert_ffn(slots, block_expert, n_used, w_gate, b_gate, w_up, b_up, w_down, b_down)
    return _combine(h1, gates, dest, y_slots, norm_final_g)


def kernel(x, norm_mix_g, w_in, attn_norm_g, rwkv_mu, rwkv_w0, rwkv_w2, rwkv_a0, rwkv_a2,
           rwkv_g2, rwkv_k_k, rwkv_k_a, rwkv_r_k, rwkv_ln_g, rwkv_ln_b, w_out, norm_ffn_g,
           w_router, b_router, moe_w_gate, moe_b_gate, moe_w_up, moe_b_up, moe_w_down,
           moe_b_down, norm_final_g):
    B, S, D = x.shape
    x2 = x.reshape(B * S, D)
    q, k, vt, km, xs = _in_projection(x2, norm_mix_g[0], w_in[0], rwkv_mu[0], B, S)
    o_att = _moba_attention(q, k, vt, km, attn_norm_g[0], B, S)
    o_rwkv = _rwkv_mix(xs, rwkv_w0[0], rwkv_w2[0], rwkv_a0[0], rwkv_a2[0], rwkv_g2[0], rwkv_k_k[0],
                       rwkv_k_a[0], rwkv_r_k[0], rwkv_ln_g[0], rwkv_ln_b[0], B, S)
    h1, xn2, idx, gates, counts = _outproj_router(o_att, o_rwkv, x2, w_out[0], norm_ffn_g[0],
                                                  w_router[0], b_router[0])
    out = _moe(h1, xn2, idx, gates, counts, moe_w_gate[0], moe_b_gate[0], moe_w_up[0], moe_b_up[0],
               moe_w_down[0], moe_b_down[0], norm_final_g)
    return out.reshape(B, S, D)
```
